```python
import math
import jax, jax.numpy as jnp
from jax import lax
import numpy as np

D_MODEL = 1024
BATCH = 2
SEQ = 8192
DEPTH = 4
DEC_BATCH = 128
DEC_SEQ = 4
PAST_LEN = 2048
PAGE_SIZE = 128

HEAD_DIM = 64
H_ATT = 8
H_SGU = 4
D_ATT = H_ATT * HEAD_DIM
D_SGU = H_SGU * HEAD_DIM
D_CONV = D_MODEL - D_ATT - D_SGU
D_MIX = D_ATT + D_SGU + D_CONV
D_IN = 3 * D_ATT + 2 * D_SGU + 2 * D_CONV
CHUNK = 128
CONV_W = 31
MOBA_BLOCK = 256
MOBA_TOPK = 3
Q_BLOCK = 128
NUM_BUCKETS = 32
MAX_DISTANCE = 128
N_EXPERTS = 64
TOP_K = 8
N_GROUPS = 8
TOPK_GROUPS = 4
D_EXPERT = 256
D_SHARED = 256
ROUTE_SCALE = 2.5
EXPERT_BLOCK = 128
ALPHA = (2 * DEPTH) ** 0.25
BETA = (8 * DEPTH) ** -0.25
LN_EPS = 1e-5

kernel_name = 'hymba_moba_sgu_conformer_moe_step'


def layer_norm(x, g, b):
    xf = x.astype(jnp.float32)
    mu = jnp.mean(xf, axis=-1, keepdims=True)
    xc = xf - mu
    var = jnp.mean(xc * xc, axis=-1, keepdims=True)
    return (xc * lax.rsqrt(var + LN_EPS) * g + b).astype(x.dtype)


def t5_bucket(dist):
    n = jnp.maximum(dist, 0)
    max_exact = NUM_BUCKETS // 2
    nf = jnp.maximum(n, 1).astype(jnp.float32)
    large = max_exact + (jnp.log(nf / max_exact) / math.log(MAX_DISTANCE / max_exact)
                         * (NUM_BUCKETS - max_exact)).astype(jnp.int32)
    large = jnp.minimum(large, NUM_BUCKETS - 1)
    return jnp.where(n < max_exact, n, large)


def moba_attention(q, k_all, v_all, q_pos, rel_bias, q_block):
    B, T, H, hd = q.shape
    L = k_all.shape[1]
    nb = -(-L // MOBA_BLOCK)
    pad = nb * MOBA_BLOCK - L
    kb = jnp.pad(k_all, ((0, 0), (0, pad), (0, 0), (0, 0))).reshape(B, nb, MOBA_BLOCK, H, hd)
    vb = jnp.pad(v_all, ((0, 0), (0, pad), (0, 0), (0, 0))).reshape(B, nb, MOBA_BLOCK, H, hd)
    k_mean = jnp.mean(kb.astype(jnp.float32), axis=2)
    kbt = kb.transpose(0, 3, 1, 2, 4)
    vbt = vb.transpose(0, 3, 1, 2, 4)
    bi = jnp.arange(B)[:, None, None, None]
    hi = jnp.arange(H)[None, :, None, None]
    blk = jnp.arange(nb)
    off = jnp.arange(MOBA_BLOCK, dtype=jnp.int32)
    bias_t = rel_bias.T
    scale = hd ** -0.5
    k_sel = min(MOBA_TOPK, nb)

    def step(args):
        qb, pb = args
        qc = pb.shape[0]
        own = pb // MOBA_BLOCK
        gate = jnp.einsum('bqhd,bnhd->bhqn', qb.astype(jnp.float32), k_mean)
        gate = jnp.where(blk[None, None, None, :] < own[None, None, :, None], gate, -jnp.inf)
        _, sel = lax.top_k(gate, k_sel)
        sel_ok = sel < own[None, None, :, None]
        own_b = jnp.broadcast_to(own[None, None, :, None], (B, H, qc, 1))
        idx = jnp.concatenate([sel, own_b], axis=-1)
        ok = jnp.concatenate([sel_ok, jnp.ones_like(own_b, dtype=bool)], axis=-1)
        ks = kbt[bi, hi, idx]
        vs = vbt[bi, hi, idx]
        dist = pb[None, None, :, None, None] - (idx[..., None] * MOBA_BLOCK + off)
        bias = bias_t[hi[..., None], t5_bucket(dist)]
        s = jnp.einsum('bqhd,bhqjkd->bhqjk', qb, ks).astype(jnp.float32) * scale + bias
        s = jnp.where(ok[..., None] & (dist >= 0), s, -jnp.inf)
        p = jax.nn.softmax(s.reshape(B, H, qc, -1), axis=-1).reshape(s.shape)
        return jnp.einsum('bhqjk,bhqjkd->bqhd', p.astype(vs.dtype), vs)

    nq = T // q_block
    qs = q.reshape(B, nq, q_block, H, hd).swapaxes(0, 1)
    ps = q_pos.reshape(nq, q_block)
    out = lax.map(step, (qs, ps))
    return out.swapaxes(0, 1).reshape(B, T, H, hd)


def spatial_gating(u, z, w_s, b_s):
    B, T = u.shape[:2]
    nc = -(-T // CHUNK)
    pad = nc * CHUNK - T
    zc = jnp.pad(z, ((0, 0), (0, pad), (0, 0), (0, 0))).reshape(B, nc, CHUNK, H_SGU, HEAD_DIM)
    causal = jnp.tril(jnp.ones((CHUNK, CHUNK), dtype=bool))
    w = jnp.where(causal, w_s, 0).astype(zc.dtype)
    s = jnp.einsum('hij,bcjhd->bcihd', w, zc) + b_s.T[None, None, :, :, None]
    s = s.reshape(B, nc * CHUNK, H_SGU, HEAD_DIM)[:, :T]
    return u * s


def conv_module(a, g, conv_prev, w_dw, b_dw, ln_g, ln_b):
    glu = a * jax.nn.sigmoid(g)
    xp = jnp.concatenate([conv_prev.astype(glu.dtype), glu], axis=1)
    y = lax.conv_general_dilated(xp, w_dw[:, None, :].astype(xp.dtype), window_strides=(1,),
                                 padding='VALID', dimension_numbers=('NWC', 'WIO', 'NWC'),
                                 feature_group_count=D_CONV) + b_dw
    y = layer_norm(y, ln_g, ln_b)
    return jax.nn.silu(y), xp[:, -(CONV_W - 1):]


def swiglu(h, w_gu, w_dn):
    gate, up = jnp.split(h @ w_gu, 2, axis=-1)
    return (jax.nn.silu(gate) * up) @ w_dn


def grouped_experts(h, eidx, gw, w_gu, w_dn):
    T, D = h.shape
    n = eidx.shape[0] * eidx.shape[1]
    G = EXPERT_BLOCK
    n_blk = (n + N_EXPERTS * (G - 1) + G - 1) // G
    rows = n_blk * G
    flat_e = eidx.reshape(-1)
    order = jnp.argsort(flat_e)
    se = flat_e[order]
    tok = (order // TOP_K).astype(jnp.int32)
    wts = gw.reshape(-1)[order]
    counts = jnp.bincount(flat_e, length=N_EXPERTS)
    start = jnp.cumsum(counts) - counts
    padded = (counts + G - 1) // G * G
    pend = jnp.cumsum(padded)
    pstart = pend - padded
    dest = pstart[se] + jnp.arange(n) - start[se]
    row_tok = jnp.full((rows,), T, dtype=jnp.int32).at[dest].set(tok)
    row_w = jnp.zeros((rows,), gw.dtype).at[dest].set(wts)
    h_pad = jnp.concatenate([h, jnp.zeros((1, D), h.dtype)], axis=0)
    xb = h_pad[row_tok].reshape(n_blk, G, D)
    blk_e = jnp.minimum(jnp.searchsorted(pend, jnp.arange(n_blk) * G, side='right'), N_EXPERTS - 1)

    def run(args):
        xg, e = args
        return swiglu(xg, w_gu[e], w_dn[e])

    yb = lax.map(run, (xb, blk_e)).reshape(rows, D)
    y = jax.ops.segment_sum(yb * row_w[:, None].astype(yb.dtype), row_tok, num_segments=T + 1)
    return y[:T].astype(h.dtype)


def moe_ffn(h, w_router, b_router, w_gu, w_dn, ws_gu, ws_dn):
    T = h.shape[0]
    scores = jax.nn.sigmoid((h @ w_router).astype(jnp.float32))
    biased = scores + b_router.astype(jnp.float32)
    grp = biased.reshape(T, N_GROUPS, N_EXPERTS // N_GROUPS)
    grp_score = jnp.sum(lax.top_k(grp, 2)[0], axis=-1)
    _, gidx = lax.top_k(grp_score, TOPK_GROUPS)
    gmask = jnp.any(gidx[..., None] == jnp.arange(N_GROUPS), axis=1)
    emask = jnp.repeat(gmask, N_EXPERTS // N_GROUPS, axis=-1)
    _, eidx = lax.top_k(jnp.where(emask, biased, -jnp.inf), TOP_K)
    gw = jnp.take_along_axis(scores, eidx, axis=-1)
    gw = gw / jnp.sum(gw, axis=-1, keepdims=True) * ROUTE_SCALE
    return grouped_experts(h, eidx, gw, w_gu, w_dn) + swiglu(h, ws_gu, ws_dn)


def mixer(x, conv_prev, past_k, past_v, q_block, rel_bias, w_in, w_out, w_s, b_s,
          sgu_g, sgu_b, w_dw, b_dw, cln_g, cln_b):
    B, T, _ = x.shape
    pos0 = past_k.shape[1]
    cuts = [D_ATT, 2 * D_ATT, 3 * D_ATT, 3 * D_ATT + D_SGU, 3 * D_ATT + 2 * D_SGU,
            3 * D_ATT + 2 * D_SGU + D_CONV]
    q, k, v, u, vz, a, g = jnp.split(x @ w_in, cuts, axis=-1)
    q = q.reshape(B, T, H_ATT, HEAD_DIM)
    k = k.reshape(B, T, H_ATT, HEAD_DIM)
    v = v.reshape(B, T, H_ATT, HEAD_DIM)
    k_all = jnp.concatenate([past_k.astype(k.dtype), k], axis=1)
    v_all = jnp.concatenate([past_v.astype(v.dtype), v], axis=1)
    q_pos = pos0 + jnp.arange(T, dtype=jnp.int32)
    att = moba_attention(q, k_all, v_all, q_pos, rel_bias, q_block).reshape(B, T, D_ATT)
    z = layer_norm(jax.nn.gelu(vz), sgu_g, sgu_b)
    sgu = spatial_gating(jax.nn.gelu(u).reshape(B, T, H_SGU, HEAD_DIM),
                         z.reshape(B, T, H_SGU, HEAD_DIM), w_s, b_s).reshape(B, T, D_SGU)
    conv, conv_buf = conv_module(a, g, conv_prev, w_dw, b_dw, cln_g, cln_b)
    out = jnp.concatenate([att, sgu, conv], axis=-1) @ w_out
    z_rows = z[:, ((T - 1) // CHUNK) * CHUNK:]
    return out, k, v, z_rows, conv_buf


def trunk_layer(x, conv_prev, past_k, past_v, q_block, rel_bias, w_in, w_out, w_s, b_s,
                sgu_g, sgu_b, w_dw, b_dw, cln_g, cln_b, ln1_g, ln1_b, w_router, b_router,
                w_exp_gu, w_exp_dn, w_sh_gu, w_sh_dn, ln2_g, ln2_b):
    m, k, v, z_rows, conv_buf = mixer(x, conv_prev, past_k, past_v, q_block, rel_bias, w_in, w_out,
                                      w_s, b_s, sgu_g, sgu_b, w_dw, b_dw, cln_g, cln_b)
    x = layer_norm(ALPHA * x + m, ln1_g, ln1_b)
    B, T, D = x.shape
    f = moe_ffn(x.reshape(B * T, D), w_router, b_router, w_exp_gu, w_exp_dn,
                w_sh_gu, w_sh_dn).reshape(B, T, D)
    x = layer_norm(ALPHA * x + f, ln2_g, ln2_b)
    return x, k, v, z_rows, conv_buf


def setup_inputs(seed: int = 0) -> dict:
    key = jax.random.key(seed)
    ks = jax.random.split(key, 32)
    n_pages = PAST_LEN // PAGE_SIZE
    n_used = DEC_BATCH * n_pages
    n_pool = n_used + n_used // 4

    def nrm(k, shape, scale):
        return jax.random.normal(k, shape, jnp.float32) * scale

    page_table = jax.random.permutation(ks[4], n_pool)[:n_used].reshape(DEC_BATCH, n_pages).astype(jnp.int32)
    return {
        'x_prompt': nrm(ks[0], (BATCH, SEQ, D_MODEL), 1.0),
        'x_sample': nrm(ks[1], (DEC_BATCH, DEC_SEQ, D_MODEL), 1.0),
        'cache_k': nrm(ks[2], (DEPTH, n_pool, PAGE_SIZE, H_ATT, HEAD_DIM), 1.0),
        'cache_v': nrm(ks[3], (DEPTH, n_pool, PAGE_SIZE, H_ATT, HEAD_DIM), 1.0),
        'page_table': page_table,
        'state_conv': nrm(ks[5], (DEPTH, DEC_BATCH, CONV_W - 1, D_CONV), 0.5),
        'rel_bias': nrm(ks[6], (NUM_BUCKETS, H_ATT), 0.5),
        'w_in': nrm(ks[7], (DEPTH, D_MODEL, D_IN), D_MODEL ** -0.5),
        'w_out': nrm(ks[8], (DEPTH, D_MIX, D_MODEL), BETA * D_MIX ** -0.5),
        'w_s': nrm(ks[9], (DEPTH, H_SGU, CHUNK, CHUNK), CHUNK ** -0.5),
        'b_s': 1.0 + nrm(ks[10], (DEPTH, H_SGU, CHUNK), 0.02),
        'sgu_ln_g': 1.0 + nrm(ks[11], (DEPTH, D_SGU), 0.02),
        'sgu_ln_b': nrm(ks[12], (DEPTH, D_SGU), 0.02),
        'w_dw': nrm(ks[13], (DEPTH, CONV_W, D_CONV), CONV_W ** -0.5),
        'b_dw': nrm(ks[14], (DEPTH, D_CONV), 0.02),
        'conv_ln_g': 1.0 + nrm(ks[15], (DEPTH, D_CONV), 0.02),
        'conv_ln_b': nrm(ks[16], (DEPTH, D_CONV), 0.02),
        'ln1_g': 1.0 + nrm(ks[17], (DEPTH, D_MODEL), 0.02),
        'ln1_b': nrm(ks[18], (DEPTH, D_MODEL), 0.02),
        'w_router': nrm(ks[19], (DEPTH, D_MODEL, N_EXPERTS), D_MODEL ** -0.5),
        'b_router': nrm(ks[20], (DEPTH, N_EXPERTS), 0.01),
        'w_exp_gu': nrm(ks[21], (DEPTH, N_EXPERTS, D_MODEL, 2 * D_EXPERT), D_MODEL ** -0.5),
        'w_exp_dn': nrm(ks[22], (DEPTH, N_EXPERTS, D_EXPERT, D_MODEL), BETA * D_EXPERT ** -0.5),
        'w_sh_gu': nrm(ks[23], (DEPTH, D_MODEL, 2 * D_SHARED), D_MODEL ** -0.5),
        'w_sh_dn': nrm(ks[24], (DEPTH, D_SHARED, D_MODEL), BETA * D_SHARED ** -0.5),
        'ln2_g': 1.0 + nrm(ks[25], (DEPTH, D_MODEL), 0.02),
        'ln2_b': nrm(ks[26], (DEPTH, D_MODEL), 0.02),
    }


def reference(x_prompt, x_sample, cache_k, cache_v, page_table, state_conv, rel_bias, w_in, w_out,
              w_s, b_s, sgu_ln_g, sgu_ln_b, w_dw, b_dw, conv_ln_g, conv_ln_b, ln1_g, ln1_b,
              w_router, b_router, w_exp_gu, w_exp_dn, w_sh_gu, w_sh_dn, ln2_g, ln2_b):
    n_prompt = x_prompt.shape[0]
    n_dec = x_sample.shape[0]
    q_block_prompt = math.gcd(Q_BLOCK, x_prompt.shape[1])
    xp, xs = x_prompt, x_sample
    kp_l, vp_l, zp_l, cp_l = [], [], [], []
    ks_l, vs_l, zs_l, cs_l = [], [], [], []
    for l in range(DEPTH):
        lw = (rel_bias, w_in[l], w_out[l], w_s[l], b_s[l], sgu_ln_g[l], sgu_ln_b[l], w_dw[l], b_dw[l],
              conv_ln_g[l], conv_ln_b[l], ln1_g[l], ln1_b[l], w_router[l], b_router[l],
              w_exp_gu[l], w_exp_dn[l], w_sh_gu[l], w_sh_dn[l], ln2_g[l], ln2_b[l])
        conv0 = jnp.zeros((n_prompt, CONV_W - 1, D_CONV), xp.dtype)
        empty = jnp.zeros((n_prompt, 0, H_ATT, HEAD_DIM), xp.dtype)
        xp, kp, vp, zp, cp = trunk_layer(xp, conv0, empty, empty, q_block_prompt, *lw)
        past_k = cache_k[l, page_table].reshape(n_dec, -1, H_ATT, HEAD_DIM)
        past_v = cache_v[l, page_table].reshape(n_dec, -1, H_ATT, HEAD_DIM)
        xs, ksn, vsn, zs, cs = trunk_layer(xs, state_conv[l], past_k, past_v, 1, *lw)
        kp_l.append(kp); vp_l.append(vp); zp_l.append(zp); cp_l.append(cp)
        ks_l.append(ksn); vs_l.append(vsn); zs_l.append(zs); cs_l.append(cs)
    return (xp, xs, jnp.stack(kp_l), jnp.stack(vp_l), jnp.stack(ks_l), jnp.stack(vs_l),
            jnp.stack(zp_l), jnp.stack(zs_l), jnp.stack(cp_l), jnp.stack(cs_l))
```

```python
import functools
import math

import jax
import jax.numpy as jnp
from jax import lax
from jax.experimental import pallas as pl
from jax.experimental.pallas import tpu as pltpu

F32 = jnp.float32
BF16 = jnp.bfloat16
I32 = jnp.int32

D_MODEL = 1024
HEAD_DIM = 64
H_ATT = 8
H_SGU = 4
D_ATT = H_ATT * HEAD_DIM
D_SGU = H_SGU * HEAD_DIM
D_CONV = D_MODEL - D_ATT - D_SGU
D_IN = 3 * D_ATT + 2 * D_SGU + 2 * D_CONV
D_REST = 2 * D_SGU + 2 * D_CONV
CHUNK = 128
CONV_W = 31
MOBA_BLOCK = 256
MOBA_TOPK = 3
Q_BLOCK = 128
NUM_BUCKETS = 32
MAX_DISTANCE = 128
N_EXPERTS = 64
TOP_K = 8
N_GROUPS = 8
GROUP_SIZE = N_EXPERTS // N_GROUPS
TOPK_GROUPS = 4
D_EXPERT = 256
ROUTE_SCALE = 2.5
LN_EPS = 1e-5
PAGE_SIZE = 128

LANES = 128
PAIR = 2 * HEAD_DIM
N_PAIRS = H_ATT // 2
NEG = -1e30
EXPERT_ROWS = 256
TOK_TILE = 128
HALO = 32
VMEM_LIMIT = 56 * 1024 * 1024

_NT = (((1,), (1,)), ((), ()))


def _cparams(sem):
    return pltpu.CompilerParams(dimension_semantics=sem, vmem_limit_bytes=VMEM_LIMIT)


def _ln(x, g, b):
    mu = jnp.mean(x, axis=-1, keepdims=True)
    xc = x - mu
    var = jnp.mean(xc * xc, axis=-1, keepdims=True)
    return xc * lax.rsqrt(var + LN_EPS) * g + b


def _t5_bucket(dist):
    n = jnp.maximum(dist, 0)
    max_exact = NUM_BUCKETS // 2
    nf = jnp.maximum(n, 1).astype(F32)
    large = max_exact + (jnp.log(nf / max_exact) / math.log(MAX_DISTANCE / max_exact)
                         * (NUM_BUCKETS - max_exact)).astype(I32)
    large = jnp.minimum(large, NUM_BUCKETS - 1)
    return jnp.where(n < max_exact, n, large)


def _inproj_kernel(x_ref, w_ref, q_ref, k_ref, v_ref, ka_ref, kb_ref, vb_ref, u_ref, km_ref,
                   *, tm, tiles_per_seq):
    i = pl.program_id(0)
    xb = x_ref[...].astype(BF16)

    def seg(lo, hi):
        return jnp.dot(xb, w_ref[:, lo:hi], preferred_element_type=F32)

    q = seg(0, D_ATT)
    q_ref[...] = (q * (HEAD_DIM ** -0.5)).astype(BF16)
    k = seg(D_ATT, 2 * D_ATT)
    k_ref[...] = k
    v = seg(2 * D_ATT, 3 * D_ATT)
    v_ref[...] = v
    vb_ref[...] = v.astype(BF16)
    u_ref[...] = seg(3 * D_ATT, D_IN)
    for c in range(tm // MOBA_BLOCK):
        km_ref[c] = jnp.mean(k[c * MOBA_BLOCK:(c + 1) * MOBA_BLOCK], axis=0, keepdims=True)
    row = lax.broadcasted_iota(I32, (tm, D_ATT), 0)
    col = lax.broadcasted_iota(I32, (tm, D_ATT), 1)
    blk = lax.rem(i, tiles_per_seq) * (tm // MOBA_BLOCK) + row // MOBA_BLOCK
    within = col % PAIR
    ka_ref[...] = jnp.where(within < HEAD_DIM, k, jnp.where(within - HEAD_DIM == blk, 1.0, 0.0)).astype(BF16)
    kb_ref[...] = jnp.where(within >= HEAD_DIM, k, jnp.where(within == blk, 1.0, 0.0)).astype(BF16)


def _inproj(x, w_in_b, tm, seq):
    r = x.shape[0]
    n = r // tm
    kern = functools.partial(_inproj_kernel, tm=tm, tiles_per_seq=max(seq // tm, 1))
    row_blk = lambda w: pl.BlockSpec((tm, w), lambda i: (i, 0))
    return pl.pallas_call(
        kern,
        grid=(n,),
        in_specs=[row_blk(D_MODEL), pl.BlockSpec((D_MODEL, D_IN), lambda i: (0, 0))],
        out_specs=[row_blk(D_ATT)] * 6 + [row_blk(D_REST),
                   pl.BlockSpec((tm // MOBA_BLOCK, 1, D_ATT), lambda i: (i, 0, 0))],
        out_shape=[jax.ShapeDtypeStruct((r, D_ATT), BF16),
                   jax.ShapeDtypeStruct((r, D_ATT), F32),
                   jax.ShapeDtypeStruct((r, D_ATT), F32),
                   jax.ShapeDtypeStruct((r, D_ATT), BF16),
                   jax.ShapeDtypeStruct((r, D_ATT), BF16),
                   jax.ShapeDtypeStruct((r, D_ATT), BF16),
                   jax.ShapeDtypeStruct((r, D_REST), F32),
                   jax.ShapeDtypeStruct((r // MOBA_BLOCK, 1, D_ATT), F32)],
        compiler_params=_cparams(("arbitrary",)),
        name="inproj",
    )(x, w_in_b)


def _select_topk_sublane(g, valid, blk, k_sel):
    sel = jnp.zeros(g.shape, F32)
    g = jnp.where(valid, g, -jnp.inf)
    for _ in range(k_sel):
        m = jnp.max(g, axis=0, keepdims=True)
        cand = jnp.where((g == m) & valid & (sel == 0.0), blk, 1 << 20)
        idx = jnp.min(cand, axis=0, keepdims=True)
        pick = cand == idx
        pick = pick & (idx < (1 << 20))
        sel = jnp.where(pick, 1.0, sel)
        g = jnp.where(pick, -jnp.inf, g)
    return sel


def _attp_kernel(q_ref, ka_ref, kb_ref, v_ref, km_ref, bt_ref, o_ref, *, nb):
    qi = pl.program_id(2)
    q_per_blk = MOBA_BLOCK // Q_BLOCK
    own = qi // q_per_blk
    qoff = lax.rem(qi, q_per_blk)
    q2 = q_ref[...]
    lane = lax.broadcasted_iota(I32, (Q_BLOCK, PAIR), 1)
    km = km_ref[0]
    lane_k = lax.broadcasted_iota(I32, (nb, PAIR), 1)
    blk = lax.broadcasted_iota(I32, (nb, Q_BLOCK), 0)
    pens = []
    for h in range(2):
        in_head = (lane_k < HEAD_DIM) if h == 0 else (lane_k >= HEAD_DIM)
        kmh = jnp.where(in_head, km, 0.0)
        hi = kmh.astype(BF16)
        lo = (kmh - hi.astype(F32)).astype(BF16)
        g = (lax.dot_general(hi, q2, _NT, preferred_element_type=F32)
             + lax.dot_general(lo, q2, _NT, preferred_element_type=F32))
        sel = _select_topk_sublane(g, blk < own, blk, min(MOBA_TOPK, nb))
        pens.append(jnp.where((sel > 0.0) | (blk == own), 0.0, NEG))
    zpad = jnp.zeros((HEAD_DIM - nb, Q_BLOCK), F32)
    pen_t = jnp.concatenate([pens[1], zpad, pens[0], zpad], axis=0)
    pen = pen_t.T
    q2f = q2.astype(F32)
    qa = (jnp.where(lane < HEAD_DIM, q2f, pen).astype(BF16),
          jnp.where(lane >= HEAD_DIM, q2f, pen).astype(BF16))
    krefs = (ka_ref, kb_ref)

    def scores(j, h):
        kt = krefs[h][pl.ds(pl.multiple_of(j * MOBA_BLOCK, MOBA_BLOCK), MOBA_BLOCK), :]
        return lax.dot_general(qa[h], kt, _NT, preferred_element_type=F32)

    def vtile(j):
        return v_ref[pl.ds(pl.multiple_of(j * MOBA_BLOCK, MOBA_BLOCK), MOBA_BLOCK), :]

    def bias(h, kind):
        return bt_ref[0, h, pl.ds(kind * q_per_blk + qoff, 1)][0]

    def update(state, s, vt):
        m, l, acc = state
        m_new = jnp.maximum(m, jnp.max(s, axis=1, keepdims=True))
        alpha = jnp.exp(m - m_new)
        p = jnp.exp(s - m_new)
        l = alpha * l + jnp.sum(p, axis=1, keepdims=True)
        acc = alpha * acc + jnp.dot(p.astype(BF16), vt, preferred_element_type=F32)
        return m_new, l, acc

    vt = vtile(own)
    state = []
    for h in range(2):
        s = scores(own, h) + bias(h, 0)
        m = jnp.max(s, axis=1, keepdims=True)
        p = jnp.exp(s - m)
        state.append((m, jnp.sum(p, axis=1, keepdims=True),
                      jnp.dot(p.astype(BF16), vt, preferred_element_type=F32)))

    def prev_body(_, st):
        j = own - 1
        vt = vtile(j)
        return tuple(update(st[h], scores(j, h) + bias(h, 1), vt) for h in range(2))

    def far_body(j, st):
        vt = vtile(j)
        return tuple(update(st[h], scores(j, h), vt) for h in range(2))

    st = lax.fori_loop(0, jnp.minimum(own, 1), prev_body, tuple(state))
    st = lax.fori_loop(0, own - 1, far_body, st)
    o0 = st[0][2] / st[0][1]
    o1 = st[1][2] / st[1][1]
    o_ref[...] = jnp.where(lane < HEAD_DIM, o0, o1).astype(o_ref.dtype)


def _att_prompt(qb, ka, kb, vb, kmean, bias_tiles, n_batch, seq):
    nb = seq // MOBA_BLOCK
    nq = seq // Q_BLOCK
    kern = functools.partial(_attp_kernel, nb=nb)
    seq_blk = pl.BlockSpec((seq, PAIR), lambda b, p, i: (b, p))
    return pl.pallas_call(
        kern,
        grid=(n_batch, N_PAIRS, nq),
        in_specs=[pl.BlockSpec((Q_BLOCK, PAIR), lambda b, p, i: (b * nq + i, p)),
                  seq_blk, seq_blk, seq_blk,
                  pl.BlockSpec((1, nb, PAIR), lambda b, p, i: (b, 0, p)),
                  pl.BlockSpec((1, 2, 4, Q_BLOCK, MOBA_BLOCK), lambda b, p, i: (p, 0, 0, 0, 0))],
        out_specs=pl.BlockSpec((Q_BLOCK, PAIR), lambda b, p, i: (b * nq + i, p)),
        out_shape=jax.ShapeDtypeStruct((n_batch * seq, D_ATT), BF16),
        compiler_params=_cparams(("arbitrary", "arbitrary", "arbitrary")),
        name="att_prompt",
    )(qb, ka, kb, vb, kmean, bias_tiles)


def _atts_kernel(pt_ref, q_ref, kn_ref, vn_ref, k0_ref, k1_ref, v0_ref, v1_ref, tl_ref, to_ref,
                 o_ref, s_buf, v_buf, g_buf, *, nbp, n_new):
    del pt_ref
    j = pl.program_id(1)
    rows = n_new * H_ATT
    r_io = lax.broadcasted_iota(I32, (rows, D_ATT), 0)
    c_io = lax.broadcasted_iota(I32, (rows, D_ATT), 1)
    diag = (c_io // HEAD_DIM) == (r_io % H_ATT)
    q = q_ref[0]
    qbd = jnp.concatenate([jnp.broadcast_to(q[i:i + 1], (H_ATT, D_ATT)) for i in range(n_new)], axis=0)
    qbd = jnp.where(diag, qbd, 0.0)
    kblk = jnp.concatenate([k0_ref[0, 0], k1_ref[0, 0]], axis=0).astype(BF16)
    s = lax.dot_general(qbd.astype(BF16), kblk, _NT, preferred_element_type=F32)
    s_buf[j] = s
    v_buf[j] = jnp.concatenate([v0_ref[0, 0], v1_ref[0, 0]], axis=0).astype(BF16)
    lane = lax.broadcasted_iota(I32, (rows, LANES), 1)
    gj = jnp.sum(s, axis=1, keepdims=True)

    @pl.when(j == 0)
    def _():
        g_buf[...] = jnp.where(lane == 0, gj, -jnp.inf)

    @pl.when(j > 0)
    def _():
        g_buf[...] = jnp.where(lane == j, gj, g_buf[...])

    @pl.when(j == nbp - 1)
    def _():
        g = g_buf[...]
        valid = lane < nbp
        sel = jnp.zeros((rows, LANES), F32)
        for _ in range(min(MOBA_TOPK, nbp + 1)):
            m = jnp.max(g, axis=1, keepdims=True)
            cand = jnp.where((g == m) & valid & (sel == 0.0), lane, 1 << 20)
            idx = jnp.min(cand, axis=1, keepdims=True)
            pick = (cand == idx) & (idx < (1 << 20))
            sel = jnp.where(pick, 1.0, sel)
            g = jnp.where(pick, -jnp.inf, g)
        pen = jnp.where(sel > 0.0, 0.0, NEG)
        kn = kn_ref[0]
        vn = vn_ref[0]
        t_own = to_ref[...]
        s_own = [jnp.sum(qbd * kn[i:i + 1], axis=1, keepdims=True) + t_own[:, i:i + 1]
                 for i in range(n_new)]
        m = s_own[0]
        for i in range(1, n_new):
            m = jnp.maximum(m, s_own[i])
        blocks = []
        for jj in range(nbp):
            sj = s_buf[jj] + pen[:, jj:jj + 1]
            if jj == nbp - 1:
                sj = sj + tl_ref[...]
            blocks.append(sj)
            m = jnp.maximum(m, jnp.max(sj, axis=1, keepdims=True))
        l = jnp.zeros((rows, 1), F32)
        acc = jnp.zeros((rows, D_ATT), F32)
        for i in range(n_new):
            p = jnp.exp(s_own[i] - m)
            l = l + p
            acc = acc + p * vn[i:i + 1]
        for jj in range(nbp):
            p = jnp.exp(blocks[jj] - m)
            l = l + jnp.sum(p, axis=1, keepdims=True)
            acc = acc + jnp.dot(p.astype(BF16), v_buf[jj], preferred_element_type=F32)
        out = jnp.where(diag, acc / l, 0.0)
        for i in range(n_new):
            o_ref[0, pl.ds(i, 1), :] = jnp.sum(out[i * H_ATT:(i + 1) * H_ATT], axis=0, keepdims=True)


def _att_sample(page_table, q_s, k_new, v_new, cache_k, cache_v, layer, t_last, t_own):
    n_dec, n_new, _ = q_s.shape
    n_pages = page_table.shape[1]
    pages_per_blk = MOBA_BLOCK // PAGE_SIZE
    nbp = n_pages // pages_per_blk
    rows = n_new * H_ATT
    kern = functools.partial(_atts_kernel, nbp=nbp, n_new=n_new)
    new_blk = pl.BlockSpec((1, n_new, D_ATT), lambda b, j, pt: (b, 0, 0))

    def page_spec(which):
        return pl.BlockSpec((1, 1, PAGE_SIZE, D_ATT),
                            lambda b, j, pt: (layer, pt[b, pages_per_blk * j + which], 0, 0))

    grid_spec = pltpu.PrefetchScalarGridSpec(
        num_scalar_prefetch=1,
        grid=(n_dec, nbp),
        in_specs=[new_blk, new_blk, new_blk, page_spec(0), page_spec(1), page_spec(0), page_spec(1),
                  pl.BlockSpec((rows, MOBA_BLOCK), lambda b, j, pt: (0, 0)),
                  pl.BlockSpec((rows, LANES), lambda b, j, pt: (0, 0))],
        out_specs=pl.BlockSpec((1, n_new, D_ATT), lambda b, j, pt: (b, 0, 0)),
        scratch_shapes=[pltpu.VMEM((nbp, rows, MOBA_BLOCK), F32),
                        pltpu.VMEM((nbp, MOBA_BLOCK, D_ATT), BF16),
                        pltpu.VMEM((rows, LANES), F32)])
    return pl.pallas_call(
        kern,
        grid_spec=grid_spec,
        out_shape=jax.ShapeDtypeStruct((n_dec, n_new, D_ATT), F32),
        compiler_params=_cparams(("arbitrary", "arbitrary")),
        name="att_sample",
    )(page_table, q_s, k_new, v_new, cache_k, cache_k, cache_v, cache_v, t_last, t_own)


def _mixp_kernel(u_ref, ws_ref, bs_ref, sg_ref, sb_ref, wdw_ref, bdw_ref, cg_ref, cb_ref,
                 sc_ref, z_ref, cbuf_ref, hist, *, tm):
    i = pl.program_id(1)
    last = pl.num_programs(1) - 1
    u = jax.nn.gelu(u_ref[:, 0:D_SGU])
    z = _ln(jax.nn.gelu(u_ref[:, D_SGU:2 * D_SGU]), sg_ref[...], sb_ref[...])
    tri_r = lax.broadcasted_iota(I32, (CHUNK, CHUNK), 0)
    tri_c = lax.broadcasted_iota(I32, (CHUNK, CHUNK), 1)
    lane = lax.broadcasted_iota(I32, (CHUNK, PAIR), 1)
    w_tril = [jnp.where(tri_c <= tri_r, ws_ref[h], 0.0).astype(BF16) for h in range(H_SGU)]
    for c in range(tm // CHUNK):
        rows = slice(c * CHUNK, (c + 1) * CHUNK)
        for pp in range(H_SGU // 2):
            cols = slice(pp * PAIR, (pp + 1) * PAIR)
            z2 = z[rows, cols].astype(BF16)
            s0 = jnp.dot(w_tril[2 * pp], z2, preferred_element_type=F32)
            s1 = jnp.dot(w_tril[2 * pp + 1], z2, preferred_element_type=F32)
            s2 = jnp.where(lane < HEAD_DIM, s0, s1) + bs_ref[:, cols]
            sc_ref[rows, cols] = (u[rows, cols] * s2).astype(sc_ref.dtype)

    @pl.when(i == last)
    def _():
        z_ref[0] = z[tm - CHUNK:tm]

    glu = u_ref[:, 2 * D_SGU:2 * D_SGU + D_CONV] * jax.nn.sigmoid(u_ref[:, 2 * D_SGU + D_CONV:D_REST])

    @pl.when(i == 0)
    def _():
        hist[0:HALO, :] = jnp.zeros((HALO, D_CONV), F32)

    hist[HALO:HALO + tm, :] = glu
    y = jnp.zeros((tm, D_CONV), F32) + bdw_ref[...]
    for w in range(CONV_W):
        y = y + hist[pl.ds(HALO - (CONV_W - 1) + w, tm), :] * wdw_ref[pl.ds(w, 1), :]
    sc_ref[:, D_SGU:D_SGU + D_CONV] = jax.nn.silu(_ln(y, cg_ref[...], cb_ref[...])).astype(sc_ref.dtype)

    @pl.when(i == last)
    def _():
        cbuf_ref[0] = hist[pl.ds(HALO + tm - (CONV_W - 1), CONV_W - 1), :]

    hist[0:HALO, :] = hist[pl.ds(tm, HALO), :]


def _mix_prompt(u_all, lw, n_batch, seq, tm):
    nt = seq // tm
    kern = functools.partial(_mixp_kernel, tm=tm)
    full = lambda shape: pl.BlockSpec(shape, lambda b, i: (0,) * len(shape))
    return pl.pallas_call(
        kern,
        grid=(n_batch, nt),
        in_specs=[pl.BlockSpec((tm, D_REST), lambda b, i: (b * nt + i, 0)),
                  full((H_SGU, CHUNK, CHUNK)), full((CHUNK, D_SGU)), full((1, D_SGU)), full((1, D_SGU)),
                  full((CONV_W, D_CONV)), full((1, D_CONV)), full((1, D_CONV)), full((1, D_CONV))],
        out_specs=[pl.BlockSpec((tm, D_SGU + D_CONV), lambda b, i: (b * nt + i, 0)),
                   pl.BlockSpec((1, CHUNK, D_SGU), lambda b, i: (b, 0, 0)),
                   pl.BlockSpec((1, CONV_W - 1, D_CONV), lambda b, i: (b, 0, 0))],
        out_shape=[jax.ShapeDtypeStruct((n_batch * seq, D_SGU + D_CONV), BF16),
                   jax.ShapeDtypeStruct((n_batch, CHUNK, D_SGU), F32),
                   jax.ShapeDtypeStruct((n_batch, CONV_W - 1, D_CONV), F32)],
        scratch_shapes=[pltpu.VMEM((HALO + tm, D_CONV), F32)],
        compiler_params=_cparams(("arbitrary", "arbitrary")),
        name="mix_prompt",
    )(u_all, lw["w_s"], lw["bs_full"], lw["sgu_g"], lw["sgu_b"], lw["w_dw"], lw["b_dw"],
      lw["cln_g"], lw["cln_b"])


def _mixs_kernel(u_ref, st_ref, wv_ref, bs_ref, sg_ref, sb_ref, wsh_ref, wdw_ref, bdw_ref, cg_ref, cb_ref,
                 sc_ref, z_ref, glu_ref, *, n_dec, n_new):
    u = jax.nn.gelu(u_ref[:, 0:D_SGU])
    z = _ln(jax.nn.gelu(u_ref[:, D_SGU:2 * D_SGU]), sg_ref[...], sb_ref[...])
    z_ref[...] = z
    glu = u_ref[:, 2 * D_SGU:2 * D_SGU + D_CONV] * jax.nn.sigmoid(u_ref[:, 2 * D_SGU + D_CONV:D_REST])
    glu_ref[...] = glu
    st = st_ref[...]
    for i in range(n_new):
        rows = slice(i * n_dec, (i + 1) * n_dec)
        s = jnp.zeros((n_dec, D_SGU), F32) + bs_ref[pl.ds(i, 1), :]
        y = jnp.sum(st * wsh_ref[i][None], axis=1) + bdw_ref[...]
        for j in range(i + 1):
            src = slice(j * n_dec, (j + 1) * n_dec)
            s = s + wv_ref[pl.ds(i * n_new + j, 1), :] * z[src]
            y = y + glu[src] * wdw_ref[pl.ds(CONV_W - 1 - i + j, 1), :]
        sc_ref[rows, 0:D_SGU] = (u[rows] * s).astype(sc_ref.dtype)
        sc_ref[rows, D_SGU:D_SGU + D_CONV] = jax.nn.silu(_ln(y, cg_ref[...], cb_ref[...])).astype(sc_ref.dtype)


def _mix_sample(u_s, state, lw, n_dec, n_new):
    r = n_dec * n_new
    kern = functools.partial(_mixs_kernel, n_dec=n_dec, n_new=n_new)
    return pl.pallas_call(
        kern,
        out_shape=[jax.ShapeDtypeStruct((r, D_SGU + D_CONV), BF16),
                   jax.ShapeDtypeStruct((r, D_SGU), F32),
                   jax.ShapeDtypeStruct((r, D_CONV), F32)],
        compiler_params=pltpu.CompilerParams(vmem_limit_bytes=VMEM_LIMIT),
        name="mix_sample",
    )(u_s, state, lw["wv_s"], lw["bs_s"], lw["sgu_g"], lw["sgu_b"], lw["wsh"], lw["w_dw"], lw["b_dw"],
      lw["cln_g"], lw["cln_b"])


def _outproj_kernel(att_ref, sc_ref, x_ref, wo_ref, g_ref, b_ref, wrh_ref, wrl_ref, br_ref,
                    x1_ref, eidx_ref, rank_ref, gw_ref, cnt_ref, carry, *, tm, alpha):
    i = pl.program_id(0)
    mix = (jnp.dot(att_ref[...], wo_ref[0:D_ATT, :], preferred_element_type=F32)
           + jnp.dot(sc_ref[...], wo_ref[D_ATT:D_MODEL, :], preferred_element_type=F32))
    x1 = _ln(alpha * x_ref[...] + mix, g_ref[...], b_ref[...])
    x1_ref[...] = x1
    hi = x1.astype(BF16)
    lo = (x1 - hi.astype(F32)).astype(BF16)
    logits = (lax.dot_general(wrh_ref[...], hi, _NT, preferred_element_type=F32)
              + lax.dot_general(wrl_ref[...], hi, _NT, preferred_element_type=F32)
              + lax.dot_general(wrh_ref[...], lo, _NT, preferred_element_type=F32))
    scores = jax.nn.sigmoid(logits)
    biased = scores + br_ref[...]
    eio = lax.broadcasted_iota(I32, (GROUP_SIZE, tm), 0)
    big = 1 << 20
    sg = [scores[g * GROUP_SIZE:(g + 1) * GROUP_SIZE] for g in range(N_GROUPS)]
    bg = [biased[g * GROUP_SIZE:(g + 1) * GROUP_SIZE] for g in range(N_GROUPS)]
    gio = lax.broadcasted_iota(I32, (N_GROUPS, tm), 0)
    gscore = jnp.zeros((N_GROUPS, tm), F32)
    for g in range(N_GROUPS):
        m1 = jnp.max(bg[g], axis=0, keepdims=True)
        i1 = jnp.min(jnp.where(bg[g] == m1, eio, big), axis=0, keepdims=True)
        m2 = jnp.max(jnp.where(eio == i1, -jnp.inf, bg[g]), axis=0, keepdims=True)
        gscore = jnp.where(gio == g, m1 + m2, gscore)
    gsel = _select_topk_sublane(gscore, gio >= 0, gio, TOPK_GROUPS)
    mg = [jnp.where(gsel[g:g + 1] > 0.0, bg[g], -jnp.inf) for g in range(N_GROUPS)]
    sio = lax.broadcasted_iota(I32, (TOP_K, tm), 0)
    eidx = jnp.zeros((TOP_K, tm), I32)
    gws = jnp.zeros((TOP_K, tm), F32)
    selm = [jnp.zeros((GROUP_SIZE, tm), F32) for _ in range(N_GROUPS)]
    for s in range(TOP_K):
        m = jnp.max(mg[0], axis=0, keepdims=True)
        for g in range(1, N_GROUPS):
            m = jnp.maximum(m, jnp.max(mg[g], axis=0, keepdims=True))
        idx = jnp.full((1, tm), big, I32)
        for g in range(N_GROUPS):
            idx = jnp.minimum(idx, jnp.min(jnp.where(mg[g] == m, eio + g * GROUP_SIZE, big),
                                           axis=0, keepdims=True))
        sc = jnp.zeros((1, tm), F32)
        for g in range(N_GROUPS):
            pick = (eio + g * GROUP_SIZE) == idx
            sc = sc + jnp.sum(jnp.where(pick, sg[g], 0.0), axis=0, keepdims=True)
            mg[g] = jnp.where(pick, -jnp.inf, mg[g])
            selm[g] = jnp.where(pick, 1.0, selm[g])
        eidx = jnp.where(sio == s, idx, eidx)
        gws = jnp.where(sio == s, sc, gws)
    gw_ref[...] = gws / jnp.sum(gws, axis=0, keepdims=True) * ROUTE_SCALE
    eidx_ref[...] = eidx

    @pl.when(i == 0)
    def _():
        carry[...] = jnp.zeros(carry.shape, F32)

    sel_all = jnp.concatenate(selm, axis=0)
    t_r = lax.broadcasted_iota(I32, (tm, tm), 0)
    t_c = lax.broadcasted_iota(I32, (tm, tm), 1)
    before = jnp.where(t_r < t_c, 1.0, 0.0).astype(BF16)
    prior = jnp.dot(sel_all.astype(BF16), before, preferred_element_type=F32) + carry[:, 0:1]
    rank = jnp.zeros((TOP_K, tm), F32)
    for s in range(TOP_K):
        r_s = jnp.zeros((1, tm), F32)
        for g in range(N_GROUPS):
            pick = (eio + g * GROUP_SIZE) == eidx[s:s + 1]
            r_s = r_s + jnp.sum(jnp.where(pick, prior[g * GROUP_SIZE:(g + 1) * GROUP_SIZE], 0.0),
                                axis=0, keepdims=True)
        rank = jnp.where(sio == s, r_s, rank)
    rank_ref[...] = rank.astype(I32)
    carry[...] = carry[...] + jnp.sum(sel_all, axis=1, keepdims=True)
    cnt_ref[...] = carry[...]


def _outproj(att, sc, x, lw, tm, alpha):
    r = x.shape[0]
    kern = functools.partial(_outproj_kernel, tm=tm, alpha=alpha)
    row_blk = lambda w: pl.BlockSpec((tm, w), lambda i: (i, 0))
    col_blk = pl.BlockSpec((TOP_K, tm), lambda i: (0, i))
    full = lambda shape: pl.BlockSpec(shape, lambda i: (0,) * len(shape))
    return pl.pallas_call(
        kern,
        grid=(r // tm,),
        in_specs=[row_blk(D_ATT), row_blk(D_SGU + D_CONV), row_blk(D_MODEL), full((D_MODEL, D_MODEL)),
                  full((1, D_MODEL)), full((1, D_MODEL)), full((N_EXPERTS, D_MODEL)),
                  full((N_EXPERTS, D_MODEL)), full((N_EXPERTS, 1))],
        out_specs=[row_blk(D_MODEL), col_blk, col_blk, col_blk, full((N_EXPERTS, LANES))],
        out_shape=[jax.ShapeDtypeStruct((r, D_MODEL), F32),
                   jax.ShapeDtypeStruct((TOP_K, r), I32),
                   jax.ShapeDtypeStruct((TOP_K, r), I32),
                   jax.ShapeDtypeStruct((TOP_K, r), F32),
                   jax.ShapeDtypeStruct((N_EXPERTS, LANES), F32)],
        scratch_shapes=[pltpu.VMEM((N_EXPERTS, LANES), F32)],
        compiler_params=_cparams(("arbitrary",)),
        name="outproj_router",
    )(att, sc, x, lw["w_out"], lw["ln1_g"], lw["ln1_b"], lw["wr_hi"], lw["wr_lo"], lw["b_router"])


def _dispatch_kernel(dest_hbm, x_ref, xs_in, xs_out, idx, sem_idx, sem):
    del xs_in
    i = pl.program_id(0)
    n = TOK_TILE * TOP_K
    cp = pltpu.make_async_copy(dest_hbm.at[i], idx, sem_idx)
    cp.start()
    cp.wait()

    def row_copy(src_row, dst_row):
        return pltpu.make_async_copy(x_ref.at[pl.ds(src_row, 1)], xs_out.at[pl.ds(dst_row, 1)], sem)

    def start(k, c):
        row_copy(k // TOP_K, idx[k]).start()
        return c

    def wait(k, c):
        row_copy(0, 0).wait()
        return c

    lax.fori_loop(0, n, start, 0)
    lax.fori_loop(0, n, wait, 0)


def _dispatch(dest_tiles, x1, xs_buf):
    r = x1.shape[0]
    return pl.pallas_call(
        _dispatch_kernel,
        grid=(r // TOK_TILE,),
        in_specs=[pl.BlockSpec(memory_space=pl.ANY),
                  pl.BlockSpec((TOK_TILE, D_MODEL), lambda i: (i, 0)),
                  pl.BlockSpec(memory_space=pl.ANY)],
        out_specs=pl.BlockSpec(memory_space=pl.ANY),
        out_shape=jax.ShapeDtypeStruct(xs_buf.shape, xs_buf.dtype),
        scratch_shapes=[pltpu.SMEM((TOK_TILE * TOP_K,), I32),
                        pltpu.SemaphoreType.DMA(()), pltpu.SemaphoreType.DMA(())],
        input_output_aliases={2: 0},
        compiler_params=_cparams(("arbitrary",)),
        name="dispatch",
    )(dest_tiles, x1, xs_buf)


def _experts_kernel(be_ref, nu_ref, xs_ref, wgu_ref, wdn_ref, ys_ref, wgu_b, wdn_b):
    i = pl.program_id(0)
    prev = be_ref[jnp.maximum(i - 1, 0)]
    fresh = (i == 0) | (be_ref[i] != prev)

    @pl.when(fresh)
    def _():
        wgu_b[...] = wgu_ref[0, 0].astype(BF16)
        wdn_b[...] = wdn_ref[0, 0].astype(BF16)

    @pl.when(i < nu_ref[0])
    def _():
        h = jnp.dot(xs_ref[...].astype(BF16), wgu_b[...], preferred_element_type=F32)
        act = jax.nn.silu(h[:, 0:D_EXPERT]) * h[:, D_EXPERT:2 * D_EXPERT]
        ys_ref[...] = jnp.dot(act.astype(BF16), wdn_b[...], preferred_element_type=F32)

    @pl.when(i >= nu_ref[0])
    def _():
        ys_ref[...] = jnp.zeros(ys_ref.shape, F32)


def _experts(blk_e, n_used, xs, w_gu, w_dn, layer):
    rows = xs.shape[0]
    grid_spec = pltpu.PrefetchScalarGridSpec(
        num_scalar_prefetch=2,
        grid=(rows // EXPERT_ROWS,),
        in_specs=[pl.BlockSpec((EXPERT_ROWS, D_MODEL), lambda i, be, nu: (i, 0)),
                  pl.BlockSpec((1, 1, D_MODEL, 2 * D_EXPERT), lambda i, be, nu: (layer, be[i], 0, 0)),
                  pl.BlockSpec((1, 1, D_EXPERT, D_MODEL), lambda i, be, nu: (layer, be[i], 0, 0))],
        out_specs=pl.BlockSpec((EXPERT_ROWS, D_MODEL), lambda i, be, nu: (i, 0)),
        scratch_shapes=[pltpu.VMEM((D_MODEL, 2 * D_EXPERT), BF16), pltpu.VMEM((D_EXPERT, D_MODEL), BF16)])
    return pl.pallas_call(
        _experts_kernel,
        grid_spec=grid_spec,
        out_shape=jax.ShapeDtypeStruct((rows, D_MODEL), F32),
        compiler_params=_cparams(("arbitrary",)),
        name="experts",
    )(blk_e, n_used, xs, w_gu, w_dn)


def _combine_kernel(dest_hbm, gw_ref, x1_ref, ys_hbm, wsg_ref, wsd_ref, g_ref, b_ref, x2_ref,
                    idx, buf, sem_idx, sem, *, alpha):
    i = pl.program_id(0)
    n = TOK_TILE * TOP_K
    cp = pltpu.make_async_copy(dest_hbm.at[i], idx, sem_idx)
    cp.start()
    cp.wait()

    def row_copy(src_row, slot, tok):
        return pltpu.make_async_copy(ys_hbm.at[pl.ds(src_row, 1)], buf.at[slot, pl.ds(tok, 1)], sem)

    def start(k, c):
        row_copy(idx[k], lax.rem(k, TOP_K), k // TOP_K).start()
        return c

    def wait(k, c):
        row_copy(0, 0, 0).wait()
        return c

    lax.fori_loop(0, n, start, 0)
    x1 = x1_ref[...]
    h = jnp.dot(x1.astype(BF16), wsg_ref[...], preferred_element_type=F32)
    act = jax.nn.silu(h[:, 0:D_EXPERT]) * h[:, D_EXPERT:2 * D_EXPERT]
    f = jnp.dot(act.astype(BF16), wsd_ref[...], preferred_element_type=F32)
    lax.fori_loop(0, n, wait, 0)
    gw = gw_ref[...]
    for s in range(TOP_K):
        f = f + gw[:, s:s + 1] * buf[s]
    x2_ref[...] = _ln(alpha * x1 + f, g_ref[...], b_ref[...])


def _combine(dest_tiles, gw_tok, x1, ys, lw, alpha):
    r = x1.shape[0]
    kern = functools.partial(_combine_kernel, alpha=alpha)
    full = lambda shape: pl.BlockSpec(shape, lambda i: (0,) * len(shape))
    return pl.pallas_call(
        kern,
        grid=(r // TOK_TILE,),
        in_specs=[pl.BlockSpec(memory_space=pl.ANY),
                  pl.BlockSpec((TOK_TILE, TOP_K), lambda i: (i, 0)),
                  pl.BlockSpec((TOK_TILE, D_MODEL), lambda i: (i, 0)),
                  pl.BlockSpec(memory_space=pl.ANY),
                  full((D_MODEL, 2 * D_EXPERT)), full((D_EXPERT, D_MODEL)),
                  full((1, D_MODEL)), full((1, D_MODEL))],
        out_specs=pl.BlockSpec((TOK_TILE, D_MODEL), lambda i: (i, 0)),
        out_shape=jax.ShapeDtypeStruct((r, D_MODEL), F32),
        scratch_shapes=[pltpu.SMEM((TOK_TILE * TOP_K,), I32),
                        pltpu.VMEM((TOP_K, TOK_TILE, D_MODEL), F32),
                        pltpu.SemaphoreType.DMA(()), pltpu.SemaphoreType.DMA(())],
        compiler_params=_cparams(("arbitrary",)),
        name="combine",
    )(dest_tiles, gw_tok, x1, ys, lw["ws_gu"], lw["ws_dn"], lw["ln2_g"], lw["ln2_b"])


def _bias_tables(rel_bias, n_new, past_len):
    far = rel_bias[_t5_bucket(jnp.asarray(MOBA_BLOCK + 1, I32))]
    qi = jnp.arange(Q_BLOCK, dtype=I32)[:, None]
    ko = jnp.arange(MOBA_BLOCK, dtype=I32)[None, :]
    tiles = []
    for kind in range(2):
        for qoff in range(MOBA_BLOCK // Q_BLOCK):
            dist = qoff * Q_BLOCK + qi - ko + kind * MOBA_BLOCK
            t = rel_bias[_t5_bucket(dist)] - far
            t = jnp.where((dist >= 0)[..., None], t, NEG)
            tiles.append(t.transpose(2, 0, 1))
    bt = jnp.stack(tiles, axis=1)
    bt = bt.reshape(N_PAIRS, 2, 4, Q_BLOCK, MOBA_BLOCK)
    i_new = jnp.repeat(jnp.arange(n_new, dtype=I32), H_ATT)[:, None]
    h_idx = jnp.tile(jnp.arange(H_ATT, dtype=I32), n_new)[:, None]
    last_start = (past_len // MOBA_BLOCK - 1) * MOBA_BLOCK
    dist_l = past_len + i_new - (last_start + ko)
    t_last = rel_bias[_t5_bucket(dist_l), h_idx] - far[h_idx]
    kn = jnp.arange(LANES, dtype=I32)[None, :]
    dist_o = i_new - kn
    t_own = rel_bias[_t5_bucket(dist_o), h_idx] - far[h_idx]
    t_own = jnp.where((dist_o >= 0) & (kn < n_new), t_own, NEG)
    return bt, t_last, t_own


def _layer_weights(l, w_in, w_out, w_s, b_s, sgu_ln_g, sgu_ln_b, w_dw, b_dw, conv_ln_g, conv_ln_b,
                   ln1_g, ln1_b, w_router, b_router, w_sh_gu, w_sh_dn, ln2_g, ln2_b, n_new):
    row = lambda a: a[l][None, :]
    ws = w_s[l]
    tril = jnp.tril(ws[:, :n_new, :n_new])
    wv_s = jnp.repeat(tril.transpose(1, 2, 0), HEAD_DIM, axis=-1).reshape(n_new * n_new, D_SGU)
    bs_s = jnp.repeat(b_s[l][:, :n_new].T, HEAD_DIM, axis=-1)
    bs_full = jnp.repeat(b_s[l].T, HEAD_DIM, axis=-1)
    r_io = jnp.arange(CONV_W - 1)[None, :]
    i_io = jnp.arange(n_new)[:, None]
    tap = r_io - i_io
    wsh = jnp.where((tap >= 0)[..., None], w_dw[l][jnp.clip(tap, 0, CONV_W - 1)], 0.0)
    wr = w_router[l].T
    wr_hi = wr.astype(BF16)
    return dict(
        w_in=w_in[l].astype(BF16), w_out=w_out[l].astype(BF16), w_s=ws, bs_full=bs_full,
        sgu_g=row(sgu_ln_g), sgu_b=row(sgu_ln_b), w_dw=w_dw[l], b_dw=row(b_dw),
        cln_g=row(conv_ln_g), cln_b=row(conv_ln_b), wv_s=wv_s, bs_s=bs_s, wsh=wsh,
        ln1_g=row(ln1_g), ln1_b=row(ln1_b), wr_hi=wr_hi, wr_lo=(wr - wr_hi.astype(F32)).astype(BF16),
        b_router=b_router[l][:, None], ws_gu=w_sh_gu[l].astype(BF16), ws_dn=w_sh_dn[l].astype(BF16),
        ln2_g=row(ln2_g), ln2_b=row(ln2_b))


@jax.jit
def _forward(x_prompt, x_sample, cache_k, cache_v, page_table, state_conv, rel_bias, w_in, w_out,
             w_s, b_s, sgu_ln_g, sgu_ln_b, w_dw, b_dw, conv_ln_g, conv_ln_b, ln1_g, ln1_b,
             w_router, b_router, w_exp_gu, w_exp_dn, w_sh_gu, w_sh_dn, ln2_g, ln2_b):
    depth = w_in.shape[0]
    n_batch, seq, _ = x_prompt.shape
    n_dec, n_new, _ = x_sample.shape
    n_pool = cache_k.shape[1]
    past_len = page_table.shape[1] * PAGE_SIZE
    alpha = (2 * depth) ** 0.25
    rp = n_batch * seq
    rs = n_dec * n_new
    r = rp + rs
    tm = 512 if (seq % 512 == 0 and rs % 512 == 0) else MOBA_BLOCK
    assert seq % tm == 0 and rs % tm == 0 and seq % MOBA_BLOCK == 0 and seq // MOBA_BLOCK <= HEAD_DIM
    assert (seq // MOBA_BLOCK) % 8 == 0 and past_len % MOBA_BLOCK == 0 and n_dec % 8 == 0
    assert MAX_DISTANCE <= MOBA_BLOCK and r % TOK_TILE == 0

    n_assign = r * TOP_K
    n_blk = (n_assign + N_EXPERTS * (EXPERT_ROWS - 1) + EXPERT_ROWS - 1) // EXPERT_ROWS
    rows_pad = n_blk * EXPERT_ROWS

    bt, t_last, t_own = _bias_tables(rel_bias, n_new, past_len)
    cache_k4 = cache_k.reshape(depth, n_pool, PAGE_SIZE, D_ATT)
    cache_v4 = cache_v.reshape(depth, n_pool, PAGE_SIZE, D_ATT)
    x = jnp.concatenate([x_prompt.reshape(rp, D_MODEL),
                         x_sample.transpose(1, 0, 2).reshape(rs, D_MODEL)], axis=0)
    xs_buf = jnp.zeros((rows_pad, D_MODEL), F32)

    def to_bi(a):
        return a.reshape(n_new, n_dec, a.shape[-1]).transpose(1, 0, 2)

    outs = [[] for _ in range(8)]
    for l in range(depth):
        lw = _layer_weights(l, w_in, w_out, w_s, b_s, sgu_ln_g, sgu_ln_b, w_dw, b_dw, conv_ln_g,
                            conv_ln_b, ln1_g, ln1_b, w_router, b_router, w_sh_gu, w_sh_dn,
                            ln2_g, ln2_b, n_new)
        qb, kf, vf, ka, kb, vb, u_all, kmean = _inproj(x, lw["w_in"], tm, seq)
        kmean_p = kmean[:rp // MOBA_BLOCK].reshape(n_batch, seq // MOBA_BLOCK, D_ATT)
        att_p = _att_prompt(qb, ka, kb, vb, kmean_p, bt, n_batch, seq)
        q_s = to_bi(qb[rp:].astype(F32))
        k_s = to_bi(kf[rp:])
        v_s = to_bi(vf[rp:])
        att_s = _att_sample(page_table, q_s, k_s, v_s, cache_k4, cache_v4, l, t_last, t_own)
        att_s = att_s.transpose(1, 0, 2).reshape(rs, D_ATT).astype(BF16)
        sc_p, z_last, cbuf_p = _mix_prompt(u_all, lw, n_batch, seq, tm)
        sc_s, z_s, glu_s = _mix_sample(u_all[rp:], state_conv[l], lw, n_dec, n_new)
        att = jnp.concatenate([att_p, att_s], axis=0)
        sc = jnp.concatenate([sc_p, sc_s], axis=0)
        x1, eidx, rank, gw, cnt = _outproj(att, sc, x, lw, tm, alpha)
        counts = cnt[:, 0].astype(I32)
        padded = (counts + EXPERT_ROWS - 1) // EXPERT_ROWS * EXPERT_ROWS
        pend = jnp.cumsum(padded)
        pstart = pend - padded
        dest = (pstart[eidx] + rank).T.reshape(r // TOK_TILE, TOK_TILE * TOP_K)
        blk_e = jnp.minimum(jnp.searchsorted(pend, jnp.arange(n_blk, dtype=I32) * EXPERT_ROWS,
                                             side="right"), N_EXPERTS - 1).astype(I32)
        n_used = (pend[-1:] // EXPERT_ROWS).astype(I32)
        xs_buf = _dispatch(dest, x1, xs_buf)
        ys = _experts(blk_e, n_used, xs_buf, w_exp_gu, w_exp_dn, l)
        x = _combine(dest, gw.T, x1, ys, lw, alpha)

        outs[0].append(kf[:rp].reshape(n_batch, seq, H_ATT, HEAD_DIM))
        outs[1].append(vf[:rp].reshape(n_batch, seq, H_ATT, HEAD_DIM))
        outs[2].append(k_s.reshape(n_dec, n_new, H_ATT, HEAD_DIM))
        outs[3].append(v_s.reshape(n_dec, n_new, H_ATT, HEAD_DIM))
        outs[4].append(z_last)
        outs[5].append(to_bi(z_s))
        outs[6].append(cbuf_p)
        outs[7].append(jnp.concatenate([state_conv[l], to_bi(glu_s)], axis=1)[:, -(CONV_W - 1):])
    y_prompt = x[:rp].reshape(n_batch, seq, D_MODEL)
    y_sample = to_bi(x[rp:])
    return (y_prompt, y_sample) + tuple(jnp.stack(o) for o in outs)


def kernel(x_prompt, x_sample, cache_k, cache_v, page_table, state_conv, rel_bias, w_in, w_out, w_s, b_s,
           sgu_ln_g, sgu_ln_b, w_dw, b_dw, conv_ln_g, conv_ln_b, ln1_g, ln1_b, w_router, b_router,
           w_exp_gu, w_exp_dn, w_sh_gu, w_sh_dn, ln2_g, ln2_b):
    return _forward(x_prompt, x_sample, cache_k, cache_v, page_table, state_conv, rel_bias, w_in, w_out,
                    w_s, b_s, sgu_ln_g, sgu_ln_b, w_dw, b_dw, conv_ln_g, conv_ln_b, ln1_g, ln1_b,
                    w_router, b_router, w_exp_gu, w_exp_dn, w_sh_gu, w_sh_dn, ln2_g, ln2_b)
```

```python
import functools
import math

import jax
import jax.numpy as jnp
from jax import lax
from jax.experimental import pallas as pl
from jax.experimental.pallas import tpu as pltpu

F32 = jnp.float32
BF16 = jnp.bfloat16
I32 = jnp.int32

D_MODEL = 1024
HEAD_DIM = 64
H_ATT = 8
H_SGU = 4
D_ATT = H_ATT * HEAD_DIM
D_SGU = H_SGU * HEAD_DIM
D_CONV = D_MODEL - D_ATT - D_SGU
D_IN = 3 * D_ATT + 2 * D_SGU + 2 * D_CONV
D_REST = 2 * D_SGU + 2 * D_CONV
CHUNK = 128
CONV_W = 31
MOBA_BLOCK = 256
MOBA_TOPK = 3
Q_BLOCK = 128
NUM_BUCKETS = 32
MAX_DISTANCE = 128
N_EXPERTS = 64
TOP_K = 8
N_GROUPS = 8
GROUP_SIZE = N_EXPERTS // N_GROUPS
TOPK_GROUPS = 4
D_EXPERT = 256
ROUTE_SCALE = 2.5
LN_EPS = 1e-5
PAGE_SIZE = 128

LANES = 128
PAIR = 2 * HEAD_DIM
N_PAIRS = H_ATT // 2
NEG = -1e30
FAR_GROUP = 4
EXPERT_ROWS = 256
TOK_TILE = 128
HALO = 32
VMEM_LIMIT = 56 * 1024 * 1024

_NT = (((1,), (1,)), ((), ()))


def _cparams(sem):
    return pltpu.CompilerParams(dimension_semantics=sem, vmem_limit_bytes=VMEM_LIMIT)


def _ln(x, g, b):
    mu = jnp.mean(x, axis=-1, keepdims=True)
    xc = x - mu
    var = jnp.mean(xc * xc, axis=-1, keepdims=True)
    return xc * lax.rsqrt(var + LN_EPS) * g + b


def _t5_bucket(dist):
    n = jnp.maximum(dist, 0)
    max_exact = NUM_BUCKETS // 2
    nf = jnp.maximum(n, 1).astype(F32)
    large = max_exact + (jnp.log(nf / max_exact) / math.log(MAX_DISTANCE / max_exact)
                         * (NUM_BUCKETS - max_exact)).astype(I32)
    large = jnp.minimum(large, NUM_BUCKETS - 1)
    return jnp.where(n < max_exact, n, large)


def _inproj_kernel(x_ref, w_ref, q_ref, k_ref, v_ref, ka_ref, kb_ref, vb_ref, u_ref, km_ref,
                   *, tm, tiles_per_seq):
    i = pl.program_id(0)
    xb = x_ref[...].astype(BF16)

    def seg(lo, hi):
        return jnp.dot(xb, w_ref[:, lo:hi], preferred_element_type=F32)

    q = seg(0, D_ATT)
    q_ref[...] = (q * (HEAD_DIM ** -0.5)).astype(BF16)
    k = seg(D_ATT, 2 * D_ATT)
    k_ref[...] = k
    v = seg(2 * D_ATT, 3 * D_ATT)
    v_ref[...] = v
    vb_ref[...] = v.astype(BF16)
    u_ref[...] = seg(3 * D_ATT, D_IN)
    for c in range(tm // MOBA_BLOCK):
        km_ref[c] = jnp.mean(k[c * MOBA_BLOCK:(c + 1) * MOBA_BLOCK], axis=0, keepdims=True)
    row = lax.broadcasted_iota(I32, (tm, D_ATT), 0)
    col = lax.broadcasted_iota(I32, (tm, D_ATT), 1)
    blk = lax.rem(i, tiles_per_seq) * (tm // MOBA_BLOCK) + row // MOBA_BLOCK
    within = col % PAIR
    ka_ref[...] = jnp.where(within < HEAD_DIM, k, jnp.where(within - HEAD_DIM == blk, 1.0, 0.0)).astype(BF16)
    kb_ref[...] = jnp.where(within >= HEAD_DIM, k, jnp.where(within == blk, 1.0, 0.0)).astype(BF16)


def _inproj(x, w_in_b, tm, seq):
    r = x.shape[0]
    n = r // tm
    kern = functools.partial(_inproj_kernel, tm=tm, tiles_per_seq=max(seq // tm, 1))
    row_blk = lambda w: pl.BlockSpec((tm, w), lambda i: (i, 0))
    return pl.pallas_call(
        kern,
        grid=(n,),
        in_specs=[row_blk(D_MODEL), pl.BlockSpec((D_MODEL, D_IN), lambda i: (0, 0))],
        out_specs=[row_blk(D_ATT)] * 6 + [row_blk(D_REST),
                   pl.BlockSpec((tm // MOBA_BLOCK, 1, D_ATT), lambda i: (i, 0, 0))],
        out_shape=[jax.ShapeDtypeStruct((r, D_ATT), BF16),
                   jax.ShapeDtypeStruct((r, D_ATT), F32),
                   jax.ShapeDtypeStruct((r, D_ATT), F32),
                   jax.ShapeDtypeStruct((r, D_ATT), BF16),
                   jax.ShapeDtypeStruct((r, D_ATT), BF16),
                   jax.ShapeDtypeStruct((r, D_ATT), BF16),
                   jax.ShapeDtypeStruct((r, D_REST), F32),
                   jax.ShapeDtypeStruct((r // MOBA_BLOCK, 1, D_ATT), F32)],
        compiler_params=_cparams(("arbitrary",)),
        name="inproj",
    )(x, w_in_b)


def _select_topk_sublane(g, valid, blk, k_sel):
    sel = jnp.zeros(g.shape, F32)
    g = jnp.where(valid, g, -jnp.inf)
    for _ in range(k_sel):
        m = jnp.max(g, axis=0, keepdims=True)
        cand = jnp.where((g == m) & valid & (sel == 0.0), blk, 1 << 20)
        idx = jnp.min(cand, axis=0, keepdims=True)
        pick = cand == idx
        pick = pick & (idx < (1 << 20))
        sel = jnp.where(pick, 1.0, sel)
        g = jnp.where(pick, -jnp.inf, g)
    return sel


def _attp_kernel(q_ref, ka_ref, kb_ref, v_ref, km_ref, bt_ref, o_ref, *, nb):
    qi = pl.program_id(2)
    q_per_blk = MOBA_BLOCK // Q_BLOCK
    own = qi // q_per_blk
    qoff = lax.rem(qi, q_per_blk)
    q2 = q_ref[...]
    lane = lax.broadcasted_iota(I32, (Q_BLOCK, PAIR), 1)
    km = km_ref[0]
    lane_k = lax.broadcasted_iota(I32, (nb, PAIR), 1)
    blk = lax.broadcasted_iota(I32, (nb, Q_BLOCK), 0)
    sels = []
    for h in range(2):
        in_head = (lane_k < HEAD_DIM) if h == 0 else (lane_k >= HEAD_DIM)
        kmh = jnp.where(in_head, km, 0.0)
        hi = kmh.astype(BF16)
        lo = (kmh - hi.astype(F32)).astype(BF16)
        g = (lax.dot_general(hi, q2, _NT, preferred_element_type=F32)
             + lax.dot_general(lo, q2, _NT, preferred_element_type=F32))
        sels.append(_select_topk_sublane(g, blk < own, blk, min(MOBA_TOPK, nb)))
    zpad = jnp.zeros((HEAD_DIM - nb, Q_BLOCK), F32)
    q2f = q2.astype(F32)

    def augment(keep):
        pens = [jnp.where(keep(sels[h]), 0.0, NEG) for h in range(2)]
        pen = jnp.concatenate([pens[1], zpad, pens[0], zpad], axis=0).T
        return (jnp.where(lane < HEAD_DIM, q2f, pen).astype(BF16),
                jnp.where(lane >= HEAD_DIM, q2f, pen).astype(BF16))

    qa_near = augment(lambda sel: (sel > 0.0) | (blk == own))
    qa_far = augment(lambda sel: (sel > 0.0) & (blk < own - 1))
    krefs = (ka_ref, kb_ref)

    def rows(j, n):
        return pl.ds(pl.multiple_of(j * MOBA_BLOCK, MOBA_BLOCK), n * MOBA_BLOCK)

    first = jnp.where(own == 0, 1, 0)
    near = rows(jnp.maximum(own - 1, 0), 2)
    vt = v_ref[near, :]
    state = []
    for h in range(2):
        s = lax.dot_general(qa_near[h], krefs[h][near, :], _NT, preferred_element_type=F32)
        s = s + bt_ref[0, h, pl.ds(first * q_per_blk + qoff, 1)][0]
        m = jnp.max(s, axis=1, keepdims=True)
        p = jnp.exp(s - m)
        state.append((m, jnp.sum(p, axis=1, keepdims=True),
                      jnp.dot(p.astype(BF16), vt, preferred_element_type=F32)))

    def far_body(g, st):
        slab = rows(g * FAR_GROUP, FAR_GROUP)
        vt = v_ref[slab, :]
        new = []
        for h in range(2):
            m, l, acc = st[h]
            s = lax.dot_general(qa_far[h], krefs[h][slab, :], _NT, preferred_element_type=F32)
            m_new = jnp.maximum(m, jnp.max(s, axis=1, keepdims=True))
            alpha = jnp.exp(m - m_new)
            p = jnp.exp(s - m_new)
            new.append((m_new, alpha * l + jnp.sum(p, axis=1, keepdims=True),
                        alpha * acc + jnp.dot(p.astype(BF16), vt, preferred_element_type=F32)))
        return tuple(new)

    n_far = jnp.maximum(own - 1, 0)
    st = lax.fori_loop(0, (n_far + FAR_GROUP - 1) // FAR_GROUP, far_body, tuple(state))
    o0 = st[0][2] / st[0][1]
    o1 = st[1][2] / st[1][1]
    o_ref[...] = jnp.where(lane < HEAD_DIM, o0, o1).astype(o_ref.dtype)


def _att_prompt(qb, ka, kb, vb, kmean, bias_tiles, n_batch, seq):
    nb = seq // MOBA_BLOCK
    nq = seq // Q_BLOCK
    kern = functools.partial(_attp_kernel, nb=nb)
    seq_blk = pl.BlockSpec((seq, PAIR), lambda b, p, i: (b, p))
    return pl.pallas_call(
        kern,
        grid=(n_batch, N_PAIRS, nq),
        in_specs=[pl.BlockSpec((Q_BLOCK, PAIR), lambda b, p, i: (b * nq + i, p)),
                  seq_blk, seq_blk, seq_blk,
                  pl.BlockSpec((1, nb, PAIR), lambda b, p, i: (b, 0, p)),
                  pl.BlockSpec((1, 2, 4, Q_BLOCK, 2 * MOBA_BLOCK), lambda b, p, i: (p, 0, 0, 0, 0))],
        out_specs=pl.BlockSpec((Q_BLOCK, PAIR), lambda b, p, i: (b * nq + i, p)),
        out_shape=jax.ShapeDtypeStruct((n_batch * seq, D_ATT), BF16),
        compiler_params=_cparams(("arbitrary", "arbitrary", "arbitrary")),
        name="att_prompt",
    )(qb, ka, kb, vb, kmean, bias_tiles)


def _atts_kernel(pt_ref, q_ref, kn_ref, vn_ref, k0_ref, k1_ref, v0_ref, v1_ref, tl_ref, to_ref,
                 o_ref, s_buf, v_buf, g_buf, *, nbp, n_new):
    del pt_ref
    j = pl.program_id(1)
    rows = n_new * H_ATT
    cols = MOBA_BLOCK * H_ATT

    def page(ref):
        return ref[0, 0].reshape(PAGE_SIZE * H_ATT, HEAD_DIM)

    qm = q_ref[0].astype(BF16)
    kblk = jnp.concatenate([page(k0_ref), page(k1_ref)], axis=0).astype(BF16)
    s = lax.dot_general(qm, kblk, _NT, preferred_element_type=F32)
    r_io = lax.broadcasted_iota(I32, (rows, cols), 0)
    c_io = lax.broadcasted_iota(I32, (rows, cols), 1)
    same_head = (c_io % H_ATT) == (r_io % H_ATT)
    s_buf[j] = jnp.where(same_head, s, NEG)
    v_buf[j] = jnp.concatenate([page(v0_ref), page(v1_ref)], axis=0).astype(BF16)
    lane = lax.broadcasted_iota(I32, (rows, LANES), 1)
    gj = jnp.sum(jnp.where(same_head, s, 0.0), axis=1, keepdims=True)

    @pl.when(j == 0)
    def _():
        g_buf[...] = jnp.where(lane == 0, gj, -jnp.inf)

    @pl.when(j > 0)
    def _():
        g_buf[...] = jnp.where(lane == j, gj, g_buf[...])

    @pl.when(j == nbp - 1)
    def _():
        g = g_buf[...]
        valid = lane < nbp
        sel = jnp.zeros((rows, LANES), F32)
        for _ in range(min(MOBA_TOPK, nbp + 1)):
            m = jnp.max(g, axis=1, keepdims=True)
            cand = jnp.where((g == m) & valid & (sel == 0.0), lane, 1 << 20)
            idx = jnp.min(cand, axis=1, keepdims=True)
            pick = (cand == idx) & (idx < (1 << 20))
            sel = jnp.where(pick, 1.0, sel)
            g = jnp.where(pick, -jnp.inf, g)
        pen = jnp.where(sel > 0.0, 0.0, NEG)
        s_own = lax.dot_general(qm, kn_ref[0].astype(BF16), _NT, preferred_element_type=F32) + to_ref[...]
        m = jnp.max(s_own, axis=1, keepdims=True)
        for jj in range(nbp):
            sj = s_buf[jj] + pen[:, jj:jj + 1]
            if jj == nbp - 1:
                sj = sj + tl_ref[...]
            s_buf[jj] = sj
            m = jnp.maximum(m, jnp.max(sj, axis=1, keepdims=True))
        p = jnp.exp(s_own - m)
        l = jnp.sum(p, axis=1, keepdims=True)
        acc = jnp.dot(p.astype(BF16), vn_ref[0].astype(BF16), preferred_element_type=F32)
        for jj in range(nbp):
            p = jnp.exp(s_buf[jj] - m)
            l = l + jnp.sum(p, axis=1, keepdims=True)
            acc = acc + jnp.dot(p.astype(BF16), v_buf[jj], preferred_element_type=F32)
        o_ref[0] = acc / l


def _att_sample(page_table, q_s, k_new, v_new, cache_k, cache_v, layer, t_last, t_own):
    n_dec, rows, _ = q_s.shape
    n_new = rows // H_ATT
    n_pages = page_table.shape[1]
    pages_per_blk = MOBA_BLOCK // PAGE_SIZE
    nbp = n_pages // pages_per_blk
    cols = MOBA_BLOCK * H_ATT
    kern = functools.partial(_atts_kernel, nbp=nbp, n_new=n_new)
    new_blk = pl.BlockSpec((1, rows, HEAD_DIM), lambda b, j, pt: (b, 0, 0))

    def page_spec(which):
        return pl.BlockSpec((1, 1, PAGE_SIZE, H_ATT, HEAD_DIM),
                            lambda b, j, pt: (layer, pt[b, pages_per_blk * j + which], 0, 0, 0))

    grid_spec = pltpu.PrefetchScalarGridSpec(
        num_scalar_prefetch=1,
        grid=(n_dec, nbp),
        in_specs=[new_blk, new_blk, new_blk, page_spec(0), page_spec(1), page_spec(0), page_spec(1),
                  pl.BlockSpec((rows, cols), lambda b, j, pt: (0, 0)),
                  pl.BlockSpec((rows, rows), lambda b, j, pt: (0, 0))],
        out_specs=new_blk,
        scratch_shapes=[pltpu.VMEM((nbp, rows, cols), F32),
                        pltpu.VMEM((nbp, cols, HEAD_DIM), BF16),
                        pltpu.VMEM((rows, LANES), F32)])
    return pl.pallas_call(
        kern,
        grid_spec=grid_spec,
        out_shape=jax.ShapeDtypeStruct((n_dec, rows, HEAD_DIM), F32),
        compiler_params=_cparams(("arbitrary", "arbitrary")),
        name="att_sample",
    )(page_table, q_s, k_new, v_new, cache_k, cache_k, cache_v, cache_v, t_last, t_own)


def _mixp_kernel(u_ref, ws_ref, bs_ref, sg_ref, sb_ref, wdw_ref, bdw_ref, cg_ref, cb_ref,
                 sc_ref, z_ref, cbuf_ref, hist, *, tm):
    i = pl.program_id(1)
    last = pl.num_programs(1) - 1
    u = jax.nn.gelu(u_ref[:, 0:D_SGU])
    z = _ln(jax.nn.gelu(u_ref[:, D_SGU:2 * D_SGU]), sg_ref[...], sb_ref[...])
    tri_r = lax.broadcasted_iota(I32, (CHUNK, CHUNK), 0)
    tri_c = lax.broadcasted_iota(I32, (CHUNK, CHUNK), 1)
    lane = lax.broadcasted_iota(I32, (CHUNK, PAIR), 1)
    w_tril = [jnp.where(tri_c <= tri_r, ws_ref[h], 0.0).astype(BF16) for h in range(H_SGU)]
    for c in range(tm // CHUNK):
        rows = slice(c * CHUNK, (c + 1) * CHUNK)
        for pp in range(H_SGU // 2):
            cols = slice(pp * PAIR, (pp + 1) * PAIR)
            z2 = z[rows, cols].astype(BF16)
            s0 = jnp.dot(w_tril[2 * pp], z2, preferred_element_type=F32)
            s1 = jnp.dot(w_tril[2 * pp + 1], z2, preferred_element_type=F32)
            s2 = jnp.where(lane < HEAD_DIM, s0, s1) + bs_ref[:, cols]
            sc_ref[rows, cols] = (u[rows, cols] * s2).astype(sc_ref.dtype)

    @pl.when(i == last)
    def _():
        z_ref[0] = z[tm - CHUNK:tm]

    glu = u_ref[:, 2 * D_SGU:2 * D_SGU + D_CONV] * jax.nn.sigmoid(u_ref[:, 2 * D_SGU + D_CONV:D_REST])

    @pl.when(i == 0)
    def _():
        hist[0:HALO, :] = jnp.zeros((HALO, D_CONV), F32)

    hist[HALO:HALO + tm, :] = glu
    y = jnp.zeros((tm, D_CONV), F32) + bdw_ref[...]
    for w in range(CONV_W):
        y = y + hist[pl.ds(HALO - (CONV_W - 1) + w, tm), :] * wdw_ref[pl.ds(w, 1), :]
    sc_ref[:, D_SGU:D_SGU + D_CONV] = jax.nn.silu(_ln(y, cg_ref[...], cb_ref[...])).astype(sc_ref.dtype)

    @pl.when(i == last)
    def _():
        cbuf_ref[0] = hist[pl.ds(HALO + tm - (CONV_W - 1), CONV_W - 1), :]

    hist[0:HALO, :] = hist[pl.ds(tm, HALO), :]


def _mix_prompt(u_all, lw, n_batch, seq, tm):
    nt = seq // tm
    kern = functools.partial(_mixp_kernel, tm=tm)
    full = lambda shape: pl.BlockSpec(shape, lambda b, i: (0,) * len(shape))
    return pl.pallas_call(
        kern,
        grid=(n_batch, nt),
        in_specs=[pl.BlockSpec((tm, D_REST), lambda b, i: (b * nt + i, 0)),
                  full((H_SGU, CHUNK, CHUNK)), full((CHUNK, D_SGU)), full((1, D_SGU)), full((1, D_SGU)),
                  full((CONV_W, D_CONV)), full((1, D_CONV)), full((1, D_CONV)), full((1, D_CONV))],
        out_specs=[pl.BlockSpec((tm, D_SGU + D_CONV), lambda b, i: (b * nt + i, 0)),
                   pl.BlockSpec((1, CHUNK, D_SGU), lambda b, i: (b, 0, 0)),
                   pl.BlockSpec((1, CONV_W - 1, D_CONV), lambda b, i: (b, 0, 0))],
        out_shape=[jax.ShapeDtypeStruct((n_batch * seq, D_SGU + D_CONV), BF16),
                   jax.ShapeDtypeStruct((n_batch, CHUNK, D_SGU), F32),
                   jax.ShapeDtypeStruct((n_batch, CONV_W - 1, D_CONV), F32)],
        scratch_shapes=[pltpu.VMEM((HALO + tm, D_CONV), F32)],
        compiler_params=_cparams(("arbitrary", "arbitrary")),
        name="mix_prompt",
    )(u_all, lw["w_s"], lw["bs_full"], lw["sgu_g"], lw["sgu_b"], lw["w_dw"], lw["b_dw"],
      lw["cln_g"], lw["cln_b"])


def _mixs_kernel(u_ref, st_ref, wv_ref, bs_ref, sg_ref, sb_ref, wsh_ref, wdw_ref, bdw_ref, cg_ref, cb_ref,
                 sc_ref, z_ref, glu_ref, *, n_dec, n_new):
    u = jax.nn.gelu(u_ref[:, 0:D_SGU])
    z = _ln(jax.nn.gelu(u_ref[:, D_SGU:2 * D_SGU]), sg_ref[...], sb_ref[...])
    z_ref[...] = z
    glu = u_ref[:, 2 * D_SGU:2 * D_SGU + D_CONV] * jax.nn.sigmoid(u_ref[:, 2 * D_SGU + D_CONV:D_REST])
    glu_ref[...] = glu
    st = st_ref[...]
    for i in range(n_new):
        rows = slice(i * n_dec, (i + 1) * n_dec)
        s = jnp.zeros((n_dec, D_SGU), F32) + bs_ref[pl.ds(i, 1), :]
        y = jnp.sum(st * wsh_ref[i][None], axis=1) + bdw_ref[...]
        for j in range(i + 1):
            src = slice(j * n_dec, (j + 1) * n_dec)
            s = s + wv_ref[pl.ds(i * n_new + j, 1), :] * z[src]
            y = y + glu[src] * wdw_ref[pl.ds(CONV_W - 1 - i + j, 1), :]
        sc_ref[rows, 0:D_SGU] = (u[rows] * s).astype(sc_ref.dtype)
        sc_ref[rows, D_SGU:D_SGU + D_CONV] = jax.nn.silu(_ln(y, cg_ref[...], cb_ref[...])).astype(sc_ref.dtype)


def _mix_sample(u_s, state, lw, n_dec, n_new):
    r = n_dec * n_new
    kern = functools.partial(_mixs_kernel, n_dec=n_dec, n_new=n_new)
    return pl.pallas_call(
        kern,
        out_shape=[jax.ShapeDtypeStruct((r, D_SGU + D_CONV), BF16),
                   jax.ShapeDtypeStruct((r, D_SGU), F32),
                   jax.ShapeDtypeStruct((r, D_CONV), F32)],
        compiler_params=pltpu.CompilerParams(vmem_limit_bytes=VMEM_LIMIT),
        name="mix_sample",
    )(u_s, state, lw["wv_s"], lw["bs_s"], lw["sgu_g"], lw["sgu_b"], lw["wsh"], lw["w_dw"], lw["b_dw"],
      lw["cln_g"], lw["cln_b"])


def _outproj_kernel(att_ref, sc_ref, x_ref, wo_ref, g_ref, b_ref, wrh_ref, wrl_ref, br_ref,
                    x1_ref, eidx_ref, rank_ref, gw_ref, cnt_ref, carry, *, tm, alpha):
    i = pl.program_id(0)
    mix = (jnp.dot(att_ref[...], wo_ref[0:D_ATT, :], preferred_element_type=F32)
           + jnp.dot(sc_ref[...], wo_ref[D_ATT:D_MODEL, :], preferred_element_type=F32))
    x1 = _ln(alpha * x_ref[...] + mix, g_ref[...], b_ref[...])
    x1_ref[...] = x1
    hi = x1.astype(BF16)
    lo = (x1 - hi.astype(F32)).astype(BF16)
    logits = (lax.dot_general(wrh_ref[...], hi, _NT, preferred_element_type=F32)
              + lax.dot_general(wrl_ref[...], hi, _NT, preferred_element_type=F32)
              + lax.dot_general(wrh_ref[...], lo, _NT, preferred_element_type=F32))
    scores = jax.nn.sigmoid(logits)
    biased = scores + br_ref[...]
    eio = lax.broadcasted_iota(I32, (GROUP_SIZE, tm), 0)
    big = 1 << 20
    sg = [scores[g * GROUP_SIZE:(g + 1) * GROUP_SIZE] for g in range(N_GROUPS)]
    bg = [biased[g * GROUP_SIZE:(g + 1) * GROUP_SIZE] for g in range(N_GROUPS)]
    gio = lax.broadcasted_iota(I32, (N_GROUPS, tm), 0)
    gscore = jnp.zeros((N_GROUPS, tm), F32)
    for g in range(N_GROUPS):
        m1 = jnp.max(bg[g], axis=0, keepdims=True)
        i1 = jnp.min(jnp.where(bg[g] == m1, eio, big), axis=0, keepdims=True)
        m2 = jnp.max(jnp.where(eio == i1, -jnp.inf, bg[g]), axis=0, keepdims=True)
        gscore = jnp.where(gio == g, m1 + m2, gscore)
    gsel = _select_topk_sublane(gscore, gio >= 0, gio, TOPK_GROUPS)
    mg = [jnp.where(gsel[g:g + 1] > 0.0, bg[g], -jnp.inf) for g in range(N_GROUPS)]
    sio = lax.broadcasted_iota(I32, (TOP_K, tm), 0)
    eidx = jnp.zeros((TOP_K, tm), I32)
    gws = jnp.zeros((TOP_K, tm), F32)
    selm = [jnp.zeros((GROUP_SIZE, tm), F32) for _ in range(N_GROUPS)]
    for s in range(TOP_K):
        m = jnp.max(mg[0], axis=0, keepdims=True)
        for g in range(1, N_GROUPS):
            m = jnp.maximum(m, jnp.max(mg[g], axis=0, keepdims=True))
        idx = jnp.full((1, tm), big, I32)
        for g in range(N_GROUPS):
            idx = jnp.minimum(idx, jnp.min(jnp.where(mg[g] == m, eio + g * GROUP_SIZE, big),
                                           axis=0, keepdims=True))
        sc = jnp.zeros((1, tm), F32)
        for g in range(N_GROUPS):
            pick = (eio + g * GROUP_SIZE) == idx
            sc = sc + jnp.sum(jnp.where(pick, sg[g], 0.0), axis=0, keepdims=True)
            mg[g] = jnp.where(pick, -jnp.inf, mg[g])
            selm[g] = jnp.where(pick, 1.0, selm[g])
        eidx = jnp.where(sio == s, idx, eidx)
        gws = jnp.where(sio == s, sc, gws)
    gw_ref[...] = gws / jnp.sum(gws, axis=0, keepdims=True) * ROUTE_SCALE
    eidx_ref[...] = eidx

    @pl.when(i == 0)
    def _():
        carry[...] = jnp.zeros(carry.shape, F32)

    sel_all = jnp.concatenate(selm, axis=0)
    t_r = lax.broadcasted_iota(I32, (tm, tm), 0)
    t_c = lax.broadcasted_iota(I32, (tm, tm), 1)
    before = jnp.where(t_r < t_c, 1.0, 0.0).astype(BF16)
    prior = jnp.dot(sel_all.astype(BF16), before, preferred_element_type=F32) + carry[:, 0:1]
    rank = jnp.zeros((TOP_K, tm), F32)
    for s in range(TOP_K):
        r_s = jnp.zeros((1, tm), F32)
        for g in range(N_GROUPS):
            pick = (eio + g * GROUP_SIZE) == eidx[s:s + 1]
            r_s = r_s + jnp.sum(jnp.where(pick, prior[g * GROUP_SIZE:(g + 1) * GROUP_SIZE], 0.0),
                                axis=0, keepdims=True)
        rank = jnp.where(sio == s, r_s, rank)
    rank_ref[...] = rank.astype(I32)
    carry[...] = carry[...] + jnp.sum(sel_all, axis=1, keepdims=True)
    cnt_ref[...] = carry[...]


def _outproj(att, sc, x, lw, tm, alpha):
    r = x.shape[0]
    kern = functools.partial(_outproj_kernel, tm=tm, alpha=alpha)
    row_blk = lambda w: pl.BlockSpec((tm, w), lambda i: (i, 0))
    col_blk = pl.BlockSpec((TOP_K, tm), lambda i: (0, i))
    full = lambda shape: pl.BlockSpec(shape, lambda i: (0,) * len(shape))
    return pl.pallas_call(
        kern,
        grid=(r // tm,),
        in_specs=[row_blk(D_ATT), row_blk(D_SGU + D_CONV), row_blk(D_MODEL), full((D_MODEL, D_MODEL)),
                  full((1, D_MODEL)), full((1, D_MODEL)), full((N_EXPERTS, D_MODEL)),
                  full((N_EXPERTS, D_MODEL)), full((N_EXPERTS, 1))],
        out_specs=[row_blk(D_MODEL), col_blk, col_blk, col_blk, full((N_EXPERTS, LANES))],
        out_shape=[jax.ShapeDtypeStruct((r, D_MODEL), F32),
                   jax.ShapeDtypeStruct((TOP_K, r), I32),
                   jax.ShapeDtypeStruct((TOP_K, r), I32),
                   jax.ShapeDtypeStruct((TOP_K, r), F32),
                   jax.ShapeDtypeStruct((N_EXPERTS, LANES), F32)],
        scratch_shapes=[pltpu.VMEM((N_EXPERTS, LANES), F32)],
        compiler_params=_cparams(("arbitrary",)),
        name="outproj_router",
    )(att, sc, x, lw["w_out"], lw["ln1_g"], lw["ln1_b"], lw["wr_hi"], lw["wr_lo"], lw["b_router"])


def _dispatch_kernel(dest_hbm, x_ref, xs_in, xs_out, idx, sem_idx, sem):
    del xs_in
    i = pl.program_id(0)
    cp = pltpu.make_async_copy(dest_hbm.at[i], idx, sem_idx)
    cp.start()
    cp.wait()

    def start(t, c):
        for s in range(TOP_K):
            pltpu.make_async_copy(x_ref.at[pl.ds(t, 1)], xs_out.at[pl.ds(idx[t * TOP_K + s], 1)], sem).start()
        return c

    lax.fori_loop(0, TOK_TILE, start, 0)
    for _ in range(TOP_K):
        pltpu.make_async_copy(x_ref, xs_out.at[pl.ds(0, TOK_TILE)], sem).wait()


def _dispatch(dest_tiles, x1, xs_buf):
    r = x1.shape[0]
    return pl.pallas_call(
        _dispatch_kernel,
        grid=(r // TOK_TILE,),
        in_specs=[pl.BlockSpec(memory_space=pl.ANY),
                  pl.BlockSpec((TOK_TILE, D_MODEL), lambda i: (i, 0)),
                  pl.BlockSpec(memory_space=pl.ANY)],
        out_specs=pl.BlockSpec(memory_space=pl.ANY),
        out_shape=jax.ShapeDtypeStruct(xs_buf.shape, xs_buf.dtype),
        scratch_shapes=[pltpu.SMEM((TOK_TILE * TOP_K,), I32),
                        pltpu.SemaphoreType.DMA(()), pltpu.SemaphoreType.DMA(())],
        input_output_aliases={2: 0},
        compiler_params=_cparams(("arbitrary",)),
        name="dispatch",
    )(dest_tiles, x1, xs_buf)


def _experts_kernel(be_ref, nu_ref, xs_ref, wgu_ref, wdn_ref, ys_ref, wgu_b, wdn_b):
    i = pl.program_id(0)
    prev = be_ref[jnp.maximum(i - 1, 0)]
    fresh = (i == 0) | (be_ref[i] != prev)

    @pl.when(fresh)
    def _():
        wgu_b[...] = wgu_ref[0, 0].astype(BF16)
        wdn_b[...] = wdn_ref[0, 0].astype(BF16)

    @pl.when(i < nu_ref[0])
    def _():
        h = jnp.dot(xs_ref[...].astype(BF16), wgu_b[...], preferred_element_type=F32)
        act = jax.nn.silu(h[:, 0:D_EXPERT]) * h[:, D_EXPERT:2 * D_EXPERT]
        ys_ref[...] = jnp.dot(act.astype(BF16), wdn_b[...], preferred_element_type=F32)

    @pl.when(i >= nu_ref[0])
    def _():
        ys_ref[...] = jnp.zeros(ys_ref.shape, F32)


def _experts(blk_e, n_used, xs, w_gu, w_dn, layer):
    rows = xs.shape[0]
    grid_spec = pltpu.PrefetchScalarGridSpec(
        num_scalar_prefetch=2,
        grid=(rows // EXPERT_ROWS,),
        in_specs=[pl.BlockSpec((EXPERT_ROWS, D_MODEL), lambda i, be, nu: (i, 0)),
                  pl.BlockSpec((1, 1, D_MODEL, 2 * D_EXPERT), lambda i, be, nu: (layer, be[i], 0, 0)),
                  pl.BlockSpec((1, 1, D_EXPERT, D_MODEL), lambda i, be, nu: (layer, be[i], 0, 0))],
        out_specs=pl.BlockSpec((EXPERT_ROWS, D_MODEL), lambda i, be, nu: (i, 0)),
        scratch_shapes=[pltpu.VMEM((D_MODEL, 2 * D_EXPERT), BF16), pltpu.VMEM((D_EXPERT, D_MODEL), BF16)])
    return pl.pallas_call(
        _experts_kernel,
        grid_spec=grid_spec,
        out_shape=jax.ShapeDtypeStruct((rows, D_MODEL), F32),
        compiler_params=_cparams(("arbitrary",)),
        name="experts",
    )(blk_e, n_used, xs, w_gu, w_dn)


def _combine_kernel(dest_hbm, gw_ref, x1_ref, ys_hbm, wsg_ref, wsd_ref, g_ref, b_ref, x2_ref,
                    idx, buf, sem_idx, sem, *, alpha):
    i = pl.program_id(0)
    cp = pltpu.make_async_copy(dest_hbm.at[i], idx, sem_idx)
    cp.start()
    cp.wait()

    def start(t, c):
        for s in range(TOP_K):
            pltpu.make_async_copy(ys_hbm.at[pl.ds(idx[t * TOP_K + s], 1)], buf.at[s, pl.ds(t, 1)], sem).start()
        return c

    lax.fori_loop(0, TOK_TILE, start, 0)
    x1 = x1_ref[...]
    h = jnp.dot(x1.astype(BF16), wsg_ref[...], preferred_element_type=F32)
    act = jax.nn.silu(h[:, 0:D_EXPERT]) * h[:, D_EXPERT:2 * D_EXPERT]
    f = jnp.dot(act.astype(BF16), wsd_ref[...], preferred_element_type=F32)
    for s in range(TOP_K):
        pltpu.make_async_copy(ys_hbm.at[pl.ds(0, TOK_TILE)], buf.at[s], sem).wait()
    gw = gw_ref[...]
    for s in range(TOP_K):
        f = f + gw[:, s:s + 1] * buf[s]
    x2_ref[...] = _ln(alpha * x1 + f, g_ref[...], b_ref[...])


def _combine(dest_tiles, gw_tok, x1, ys, lw, alpha):
    r = x1.shape[0]
    kern = functools.partial(_combine_kernel, alpha=alpha)
    full = lambda shape: pl.BlockSpec(shape, lambda i: (0,) * len(shape))
    return pl.pallas_call(
        kern,
        grid=(r // TOK_TILE,),
        in_specs=[pl.BlockSpec(memory_space=pl.ANY),
                  pl.BlockSpec((TOK_TILE, TOP_K), lambda i: (i, 0)),
                  pl.BlockSpec((TOK_TILE, D_MODEL), lambda i: (i, 0)),
                  pl.BlockSpec(memory_space=pl.ANY),
                  full((D_MODEL, 2 * D_EXPERT)), full((D_EXPERT, D_MODEL)),
                  full((1, D_MODEL)), full((1, D_MODEL))],
        out_specs=pl.BlockSpec((TOK_TILE, D_MODEL), lambda i: (i, 0)),
        out_shape=jax.ShapeDtypeStruct((r, D_MODEL), F32),
        scratch_shapes=[pltpu.SMEM((TOK_TILE * TOP_K,), I32),
                        pltpu.VMEM((TOP_K, TOK_TILE, D_MODEL), F32),
                        pltpu.SemaphoreType.DMA(()), pltpu.SemaphoreType.DMA(())],
        compiler_params=_cparams(("arbitrary",)),
        name="combine",
    )(dest_tiles, gw_tok, x1, ys, lw["ws_gu"], lw["ws_dn"], lw["ln2_g"], lw["ln2_b"])


def _bias_tables(rel_bias, n_new, past_len):
    far = rel_bias[_t5_bucket(jnp.asarray(MOBA_BLOCK + 1, I32))]
    qi = jnp.arange(Q_BLOCK, dtype=I32)[:, None]
    ko = jnp.arange(MOBA_BLOCK, dtype=I32)[None, :]
    k2 = jnp.arange(2 * MOBA_BLOCK, dtype=I32)[None, :]
    tiles = []
    for first in range(2):
        for qoff in range(MOBA_BLOCK // Q_BLOCK):
            dist = qoff * Q_BLOCK + qi - k2 + (1 - first) * MOBA_BLOCK
            t = rel_bias[_t5_bucket(dist)] - far
            t = jnp.where((dist >= 0)[..., None], t, NEG)
            tiles.append(t.transpose(2, 0, 1))
    bt = jnp.stack(tiles, axis=1)
    bt = bt.reshape(N_PAIRS, 2, 4, Q_BLOCK, 2 * MOBA_BLOCK)
    i_new = jnp.repeat(jnp.arange(n_new, dtype=I32), H_ATT)[:, None]
    h_idx = jnp.tile(jnp.arange(H_ATT, dtype=I32), n_new)[:, None]
    last_start = (past_len // MOBA_BLOCK - 1) * MOBA_BLOCK
    dist_l = past_len + i_new - (last_start + ko)
    t_last = rel_bias[_t5_bucket(dist_l), h_idx] - far[h_idx]
    t_last = jnp.repeat(t_last, H_ATT, axis=1)
    dist_o = i_new - i_new.T
    t_own = rel_bias[_t5_bucket(dist_o), h_idx] - far[h_idx]
    t_own = jnp.where((dist_o >= 0) & (h_idx == h_idx.T), t_own, NEG)
    return bt, t_last, t_own


def _layer_weights(l, w_in, w_out, w_s, b_s, sgu_ln_g, sgu_ln_b, w_dw, b_dw, conv_ln_g, conv_ln_b,
                   ln1_g, ln1_b, w_router, b_router, w_sh_gu, w_sh_dn, ln2_g, ln2_b, n_new):
    row = lambda a: a[l][None, :]
    ws = w_s[l]
    tril = jnp.tril(ws[:, :n_new, :n_new])
    wv_s = jnp.repeat(tril.transpose(1, 2, 0), HEAD_DIM, axis=-1).reshape(n_new * n_new, D_SGU)
    bs_s = jnp.repeat(b_s[l][:, :n_new].T, HEAD_DIM, axis=-1)
    bs_full = jnp.repeat(b_s[l].T, HEAD_DIM, axis=-1)
    r_io = jnp.arange(CONV_W - 1)[None, :]
    i_io = jnp.arange(n_new)[:, None]
    tap = r_io - i_io
    wsh = jnp.where((tap >= 0)[..., None], w_dw[l][jnp.clip(tap, 0, CONV_W - 1)], 0.0)
    wr = w_router[l].T
    wr_hi = wr.astype(BF16)
    return dict(
        w_in=w_in[l].astype(BF16), w_out=w_out[l].astype(BF16), w_s=ws, bs_full=bs_full,
        sgu_g=row(sgu_ln_g), sgu_b=row(sgu_ln_b), w_dw=w_dw[l], b_dw=row(b_dw),
        cln_g=row(conv_ln_g), cln_b=row(conv_ln_b), wv_s=wv_s, bs_s=bs_s, wsh=wsh,
        ln1_g=row(ln1_g), ln1_b=row(ln1_b), wr_hi=wr_hi, wr_lo=(wr - wr_hi.astype(F32)).astype(BF16),
        b_router=b_router[l][:, None], ws_gu=w_sh_gu[l].astype(BF16), ws_dn=w_sh_dn[l].astype(BF16),
        ln2_g=row(ln2_g), ln2_b=row(ln2_b))


@jax.jit
def _forward(x_prompt, x_sample, cache_k, cache_v, page_table, state_conv, rel_bias, w_in, w_out,
             w_s, b_s, sgu_ln_g, sgu_ln_b, w_dw, b_dw, conv_ln_g, conv_ln_b, ln1_g, ln1_b,
             w_router, b_router, w_exp_gu, w_exp_dn, w_sh_gu, w_sh_dn, ln2_g, ln2_b):
    depth = w_in.shape[0]
    n_batch, seq, _ = x_prompt.shape
    n_dec, n_new, _ = x_sample.shape
    n_pool = cache_k.shape[1]
    past_len = page_table.shape[1] * PAGE_SIZE
    alpha = (2 * depth) ** 0.25
    rp = n_batch * seq
    rs = n_dec * n_new
    r = rp + rs
    tm = 512 if (seq % 512 == 0 and rs % 512 == 0) else MOBA_BLOCK
    assert seq % tm == 0 and rs % tm == 0 and seq % MOBA_BLOCK == 0 and seq // MOBA_BLOCK <= HEAD_DIM
    assert (seq // MOBA_BLOCK) % 8 == 0 and past_len % MOBA_BLOCK == 0 and n_dec % 8 == 0
    assert MAX_DISTANCE <= MOBA_BLOCK and r % TOK_TILE == 0
    assert (seq // MOBA_BLOCK) % FAR_GROUP == 0 and seq >= 2 * MOBA_BLOCK
    del n_pool

    n_assign = r * TOP_K
    n_blk = (n_assign + N_EXPERTS * (EXPERT_ROWS - 1) + EXPERT_ROWS - 1) // EXPERT_ROWS
    rows_pad = n_blk * EXPERT_ROWS
    sample_rows = (n_dec, n_new * H_ATT, HEAD_DIM)

    bt, t_last, t_own = _bias_tables(rel_bias, n_new, past_len)
    x = jnp.concatenate([x_prompt.reshape(rp, D_MODEL),
                         x_sample.transpose(1, 0, 2).reshape(rs, D_MODEL)], axis=0)
    xs_buf = jnp.zeros((rows_pad, D_MODEL), F32)

    def to_bi(a):
        return a.reshape(n_new, n_dec, a.shape[-1]).transpose(1, 0, 2)

    outs = [[] for _ in range(8)]
    for l in range(depth):
        lw = _layer_weights(l, w_in, w_out, w_s, b_s, sgu_ln_g, sgu_ln_b, w_dw, b_dw, conv_ln_g,
                            conv_ln_b, ln1_g, ln1_b, w_router, b_router, w_sh_gu, w_sh_dn,
                            ln2_g, ln2_b, n_new)
        qb, kf, vf, ka, kb, vb, u_all, kmean = _inproj(x, lw["w_in"], tm, seq)
        kmean_p = kmean[:rp // MOBA_BLOCK].reshape(n_batch, seq // MOBA_BLOCK, D_ATT)
        att_p = _att_prompt(qb, ka, kb, vb, kmean_p, bt, n_batch, seq)
        q_s = to_bi(qb[rp:].astype(F32))
        k_s = to_bi(kf[rp:])
        v_s = to_bi(vf[rp:])
        att_s = _att_sample(page_table, q_s.reshape(sample_rows), k_s.reshape(sample_rows),
                            v_s.reshape(sample_rows), cache_k, cache_v, l, t_last, t_own)
        att_s = att_s.reshape(n_dec, n_new, D_ATT).transpose(1, 0, 2).reshape(rs, D_ATT).astype(BF16)
        sc_p, z_last, cbuf_p = _mix_prompt(u_all, lw, n_batch, seq, tm)
        sc_s, z_s, glu_s = _mix_sample(u_all[rp:], state_conv[l], lw, n_dec, n_new)
        att = jnp.concatenate([att_p, att_s], axis=0)
        sc = jnp.concatenate([sc_p, sc_s], axis=0)
        x1, eidx, rank, gw, cnt = _outproj(att, sc, x, lw, tm, alpha)
        counts = cnt[:, 0].astype(I32)
        padded = (counts + EXPERT_ROWS - 1) // EXPERT_ROWS * EXPERT_ROWS
        pend = jnp.cumsum(padded)
        pstart = pend - padded
        e_io = jnp.arange(N_EXPERTS, dtype=I32)
        pstart_of = jnp.sum(jnp.where(eidx[..., None] == e_io, pstart, 0), axis=-1)
        dest = (pstart_of + rank).T.reshape(r // TOK_TILE, TOK_TILE * TOP_K)
        blk_rows = jnp.arange(n_blk, dtype=I32)[:, None] * EXPERT_ROWS
        blk_e = jnp.minimum(jnp.sum((pend[None, :] <= blk_rows).astype(I32), axis=1), N_EXPERTS - 1)
        n_used = (pend[-1:] // EXPERT_ROWS).astype(I32)
        xs_buf = _dispatch(dest, x1, xs_buf)
        ys = _experts(blk_e, n_used, xs_buf, w_exp_gu, w_exp_dn, l)
        x = _combine(dest, gw.T, x1, ys, lw, alpha)

        outs[0].append(kf[:rp].reshape(n_batch, seq, H_ATT, HEAD_DIM))
        outs[1].append(vf[:rp].reshape(n_batch, seq, H_ATT, HEAD_DIM))
        outs[2].append(k_s.reshape(n_dec, n_new, H_ATT, HEAD_DIM))
        outs[3].append(v_s.reshape(n_dec, n_new, H_ATT, HEAD_DIM))
        outs[4].append(z_last)
        outs[5].append(to_bi(z_s))
        outs[6].append(cbuf_p)
        outs[7].append(jnp.concatenate([state_conv[l], to_bi(glu_s)], axis=1)[:, -(CONV_W - 1):])
    y_prompt = x[:rp].reshape(n_batch, seq, D_MODEL)
    y_sample = to_bi(x[rp:])
    return (y_prompt, y_sample) + tuple(jnp.stack(o) for o in outs)


def kernel(x_prompt, x_sample, cache_k, cache_v, page_table, state_conv, rel_bias, w_in, w_out, w_s, b_s,
           sgu_ln_g, sgu_ln_b, w_dw, b_dw, conv_ln_g, conv_ln_b, ln1_g, ln1_b, w_router, b_router,
           w_exp_gu, w_exp_dn, w_sh_gu, w_sh_dn, ln2_g, ln2_b):
    return _forward(x_prompt, x_sample, cache_k, cache_v, page_table, state_conv, rel_bias, w_in, w_out,
                    w_s, b_s, sgu_ln_g, sgu_ln_b, w_dw, b_dw, conv_ln_g, conv_ln_b, ln1_g, ln1_b,
                    w_router, b_router, w_exp_gu, w_exp_dn, w_sh_gu, w_sh_dn, ln2_g, ln2_b)
```

```python
import functools
import math

import jax
import jax.numpy as jnp
from jax import lax
from jax.experimental import pallas as pl
from jax.experimental.pallas import tpu as pltpu

F32 = jnp.float32
BF16 = jnp.bfloat16
I32 = jnp.int32

D_MODEL = 1024
HEAD_DIM = 64
H_ATT = 8
H_SGU = 4
D_ATT = H_ATT * HEAD_DIM
D_SGU = H_SGU * HEAD_DIM
D_CONV = D_MODEL - D_ATT - D_SGU
D_IN = 3 * D_ATT + 2 * D_SGU + 2 * D_CONV
D_REST = 2 * D_SGU + 2 * D_CONV
CHUNK = 128
CONV_W = 31
MOBA_BLOCK = 256
MOBA_TOPK = 3
Q_BLOCK = 256
NUM_BUCKETS = 32
MAX_DISTANCE = 128
N_EXPERTS = 64
TOP_K = 8
N_GROUPS = 8
GROUP_SIZE = N_EXPERTS // N_GROUPS
TOPK_GROUPS = 4
D_EXPERT = 256
ROUTE_SCALE = 2.5
LN_EPS = 1e-5
PAGE_SIZE = 128

LANES = 128
PAIR = 2 * HEAD_DIM
N_PAIRS = H_ATT // 2
NEG = -1e30
FAR_GROUP = 4
NEW_PAD = 8
EXPERT_ROWS = 256
TOK_TILE = 128
HALO = 32
VMEM_LIMIT = 56 * 1024 * 1024

_NT = (((1,), (1,)), ((), ()))


def _cparams(sem):
    return pltpu.CompilerParams(dimension_semantics=sem, vmem_limit_bytes=VMEM_LIMIT)


def _ln(x, g, b):
    mu = jnp.mean(x, axis=-1, keepdims=True)
    xc = x - mu
    var = jnp.mean(xc * xc, axis=-1, keepdims=True)
    return xc * lax.rsqrt(var + LN_EPS) * g + b


def _t5_bucket(dist):
    n = jnp.maximum(dist, 0)
    max_exact = NUM_BUCKETS // 2
    nf = jnp.maximum(n, 1).astype(F32)
    large = max_exact + (jnp.log(nf / max_exact) / math.log(MAX_DISTANCE / max_exact)
                         * (NUM_BUCKETS - max_exact)).astype(I32)
    large = jnp.minimum(large, NUM_BUCKETS - 1)
    return jnp.where(n < max_exact, n, large)


def _inproj_kernel(x_ref, w_ref, q_ref, k_ref, v_ref, ka_ref, kb_ref, vb_ref, u_ref, km_ref,
                   *, tm, tiles_per_seq):
    i = pl.program_id(0)
    xb = x_ref[...].astype(BF16)

    def seg(lo, hi):
        return jnp.dot(xb, w_ref[:, lo:hi], preferred_element_type=F32)

    q = seg(0, D_ATT)
    q_ref[...] = (q * (HEAD_DIM ** -0.5)).astype(BF16)
    k = seg(D_ATT, 2 * D_ATT)
    k_ref[...] = k
    v = seg(2 * D_ATT, 3 * D_ATT)
    v_ref[...] = v
    vb_ref[...] = v.astype(BF16)
    u_ref[...] = seg(3 * D_ATT, D_IN)
    for c in range(tm // MOBA_BLOCK):
        km_ref[c] = jnp.mean(k[c * MOBA_BLOCK:(c + 1) * MOBA_BLOCK], axis=0, keepdims=True)
    row = lax.broadcasted_iota(I32, (tm, D_ATT), 0)
    col = lax.broadcasted_iota(I32, (tm, D_ATT), 1)
    blk = lax.rem(i, tiles_per_seq) * (tm // MOBA_BLOCK) + row // MOBA_BLOCK
    within = col % PAIR
    ka_ref[...] = jnp.where(within < HEAD_DIM, k, jnp.where(within - HEAD_DIM == blk, 1.0, 0.0)).astype(BF16)
    kb_ref[...] = jnp.where(within >= HEAD_DIM, k, jnp.where(within == blk, 1.0, 0.0)).astype(BF16)


def _inproj(x, w_in_b, tm, seq):
    r = x.shape[0]
    n = r // tm
    kern = functools.partial(_inproj_kernel, tm=tm, tiles_per_seq=max(seq // tm, 1))
    row_blk = lambda w: pl.BlockSpec((tm, w), lambda i: (i, 0))
    return pl.pallas_call(
        kern,
        grid=(n,),
        in_specs=[row_blk(D_MODEL), pl.BlockSpec((D_MODEL, D_IN), lambda i: (0, 0))],
        out_specs=[row_blk(D_ATT)] * 6 + [row_blk(D_REST),
                   pl.BlockSpec((tm // MOBA_BLOCK, 1, D_ATT), lambda i: (i, 0, 0))],
        out_shape=[jax.ShapeDtypeStruct((r, D_ATT), BF16),
                   jax.ShapeDtypeStruct((r, D_ATT), F32),
                   jax.ShapeDtypeStruct((r, D_ATT), F32),
                   jax.ShapeDtypeStruct((r, D_ATT), BF16),
                   jax.ShapeDtypeStruct((r, D_ATT), BF16),
                   jax.ShapeDtypeStruct((r, D_ATT), BF16),
                   jax.ShapeDtypeStruct((r, D_REST), F32),
                   jax.ShapeDtypeStruct((r // MOBA_BLOCK, 1, D_ATT), F32)],
        compiler_params=_cparams(("arbitrary",)),
        name="inproj",
    )(x, w_in_b)


def _select_topk_sublane(g, valid, blk, k_sel):
    sel = jnp.zeros(g.shape, F32)
    g = jnp.where(valid, g, -jnp.inf)
    for _ in range(k_sel):
        m = jnp.max(g, axis=0, keepdims=True)
        cand = jnp.where((g == m) & valid & (sel == 0.0), blk, 1 << 20)
        idx = jnp.min(cand, axis=0, keepdims=True)
        pick = cand == idx
        pick = pick & (idx < (1 << 20))
        sel = jnp.where(pick, 1.0, sel)
        g = jnp.where(pick, -jnp.inf, g)
    return sel


def _attp_kernel(q_ref, ka_ref, kb_ref, v_ref, km_ref, bt_ref, o_ref, *, nb):
    qi = pl.program_id(2)
    q_per_blk = MOBA_BLOCK // Q_BLOCK
    own = qi // q_per_blk
    qoff = lax.rem(qi, q_per_blk)
    q2 = q_ref[...]
    lane = lax.broadcasted_iota(I32, (Q_BLOCK, PAIR), 1)
    km = km_ref[0]
    lane_k = lax.broadcasted_iota(I32, (nb, PAIR), 1)
    blk = lax.broadcasted_iota(I32, (nb, Q_BLOCK), 0)
    sels = []
    for h in range(2):
        in_head = (lane_k < HEAD_DIM) if h == 0 else (lane_k >= HEAD_DIM)
        kmh = jnp.where(in_head, km, 0.0)
        hi = kmh.astype(BF16)
        lo = (kmh - hi.astype(F32)).astype(BF16)
        g = (lax.dot_general(hi, q2, _NT, preferred_element_type=F32)
             + lax.dot_general(lo, q2, _NT, preferred_element_type=F32))
        sels.append(_select_topk_sublane(g, blk < own, blk, min(MOBA_TOPK, nb)))
    zpad = jnp.zeros((HEAD_DIM - nb, Q_BLOCK), F32)
    q2f = q2.astype(F32)

    def augment(keep):
        pens = [jnp.where(keep(sels[h]), 0.0, NEG) for h in range(2)]
        pen = jnp.concatenate([pens[1], zpad, pens[0], zpad], axis=0).T
        return (jnp.where(lane < HEAD_DIM, q2f, pen).astype(BF16),
                jnp.where(lane >= HEAD_DIM, q2f, pen).astype(BF16))

    qa_near = augment(lambda sel: (sel > 0.0) | (blk == own))
    qa_far = augment(lambda sel: (sel > 0.0) & (blk < own - 1))
    krefs = (ka_ref, kb_ref)

    def rows(j, n):
        return pl.ds(pl.multiple_of(j * MOBA_BLOCK, MOBA_BLOCK), n * MOBA_BLOCK)

    first = jnp.where(own == 0, 1, 0)
    near = rows(jnp.maximum(own - 1, 0), 2)
    vt = v_ref[near, :]
    state = []
    for h in range(2):
        s = lax.dot_general(qa_near[h], krefs[h][near, :], _NT, preferred_element_type=F32)
        s = s + bt_ref[0, h, pl.ds(first * q_per_blk + qoff, 1)][0]
        m = jnp.max(s, axis=1, keepdims=True)
        p = jnp.exp(s - m)
        state.append((m, jnp.sum(p, axis=1, keepdims=True),
                      jnp.dot(p.astype(BF16), vt, preferred_element_type=F32)))

    def far_body(g, st):
        slab = rows(g * FAR_GROUP, FAR_GROUP)
        vt = v_ref[slab, :]
        new = []
        for h in range(2):
            m, l, acc = st[h]
            s = lax.dot_general(qa_far[h], krefs[h][slab, :], _NT, preferred_element_type=F32)
            m_new = jnp.maximum(m, jnp.max(s, axis=1, keepdims=True))
            alpha = jnp.exp(m - m_new)
            p = jnp.exp(s - m_new)
            new.append((m_new, alpha * l + jnp.sum(p, axis=1, keepdims=True),
                        alpha * acc + jnp.dot(p.astype(BF16), vt, preferred_element_type=F32)))
        return tuple(new)

    n_far = jnp.maximum(own - 1, 0)
    st = lax.fori_loop(0, (n_far + FAR_GROUP - 1) // FAR_GROUP, far_body, tuple(state))
    o0 = st[0][2] / st[0][1]
    o1 = st[1][2] / st[1][1]
    o_ref[...] = jnp.where(lane < HEAD_DIM, o0, o1).astype(o_ref.dtype)


def _att_prompt(qb, ka, kb, vb, kmean, bias_tiles, n_batch, seq):
    nb = seq // MOBA_BLOCK
    nq = seq // Q_BLOCK
    kern = functools.partial(_attp_kernel, nb=nb)
    seq_blk = pl.BlockSpec((seq, PAIR), lambda b, p, i: (b, p))
    return pl.pallas_call(
        kern,
        grid=(n_batch, N_PAIRS, nq),
        in_specs=[pl.BlockSpec((Q_BLOCK, PAIR), lambda b, p, i: (b * nq + i, p)),
                  seq_blk, seq_blk, seq_blk,
                  pl.BlockSpec((1, nb, PAIR), lambda b, p, i: (b, 0, p)),
                  pl.BlockSpec((1, 2, 2 * (MOBA_BLOCK // Q_BLOCK), Q_BLOCK, 2 * MOBA_BLOCK),
                               lambda b, p, i: (p, 0, 0, 0, 0))],
        out_specs=pl.BlockSpec((Q_BLOCK, PAIR), lambda b, p, i: (b * nq + i, p)),
        out_shape=jax.ShapeDtypeStruct((n_batch * seq, D_ATT), BF16),
        compiler_params=_cparams(("arbitrary", "arbitrary", "arbitrary")),
        name="att_prompt",
    )(qb, ka, kb, vb, kmean, bias_tiles)


def _atts_kernel(pt_ref, q_ref, kn_ref, vn_ref, k0_ref, k1_ref, v0_ref, v1_ref, tl_ref, to_ref,
                 o_ref, s_buf, v_buf, g_buf, *, nbp, n_new):
    del pt_ref
    j = pl.program_id(1)
    hqk = (((2,), (1,)), ((0,), (0,)))
    hpv = (((2,), (2,)), ((0,), (0,)))
    q = q_ref[0]
    qb = q.astype(BF16)
    kt = jnp.concatenate([k0_ref[0, 0], k1_ref[0, 0]], axis=2).astype(BF16)
    s = lax.dot_general(qb, kt, hqk, preferred_element_type=F32)
    s_buf[j] = s
    v_buf[j] = jnp.concatenate([v0_ref[0, 0], v1_ref[0, 0]], axis=2).astype(BF16)
    lane = lax.broadcasted_iota(I32, (H_ATT, NEW_PAD, LANES), 2)
    gj = jnp.sum(s, axis=2, keepdims=True)

    @pl.when(j == 0)
    def _():
        g_buf[...] = jnp.where(lane == 0, gj, -jnp.inf)

    @pl.when(j > 0)
    def _():
        g_buf[...] = jnp.where(lane == j, gj, g_buf[...])

    @pl.when(j == nbp - 1)
    def _():
        g = g_buf[...]
        valid = lane < nbp
        sel = jnp.zeros(g.shape, F32)
        for _ in range(min(MOBA_TOPK, nbp + 1)):
            m = jnp.max(g, axis=2, keepdims=True)
            cand = jnp.where((g == m) & valid & (sel == 0.0), lane, 1 << 20)
            idx = jnp.min(cand, axis=2, keepdims=True)
            pick = (cand == idx) & (idx < (1 << 20))
            sel = jnp.where(pick, 1.0, sel)
            g = jnp.where(pick, -jnp.inf, g)
        pen = jnp.where(sel > 0.0, 0.0, NEG)
        kn = kn_ref[0]
        vn = vn_ref[0]
        t_own = to_ref[...]
        s_own = [jnp.sum(q * kn[:, i:i + 1, :], axis=2, keepdims=True) + t_own[:, :, i:i + 1]
                 for i in range(n_new)]
        m = s_own[0]
        for i in range(1, n_new):
            m = jnp.maximum(m, s_own[i])
        for jj in range(nbp):
            sj = s_buf[jj] + pen[:, :, jj:jj + 1]
            if jj == nbp - 1:
                sj = sj + tl_ref[...]
            s_buf[jj] = sj
            m = jnp.maximum(m, jnp.max(sj, axis=2, keepdims=True))
        l = jnp.zeros(m.shape, F32)
        acc = jnp.zeros(q.shape, F32)
        for i in range(n_new):
            p = jnp.exp(s_own[i] - m)
            l = l + p
            acc = acc + p * vn[:, i:i + 1, :]
        for jj in range(nbp):
            p = jnp.exp(s_buf[jj] - m)
            l = l + jnp.sum(p, axis=2, keepdims=True)
            acc = acc + lax.dot_general(p.astype(BF16), v_buf[jj], hpv, preferred_element_type=F32)
        o_ref[0] = acc / l


def _att_sample(page_table, q_s, k_new, v_new, cache_kt, cache_vt, layer, t_last, t_own, n_new):
    n_dec = q_s.shape[0]
    n_pages = page_table.shape[1]
    pages_per_blk = MOBA_BLOCK // PAGE_SIZE
    nbp = n_pages // pages_per_blk
    kern = functools.partial(_atts_kernel, nbp=nbp, n_new=n_new)
    new_blk = pl.BlockSpec((1, H_ATT, NEW_PAD, HEAD_DIM), lambda b, j, pt: (b, 0, 0, 0))

    def page_spec(which):
        return pl.BlockSpec((1, 1, H_ATT, HEAD_DIM, PAGE_SIZE),
                            lambda b, j, pt: (layer, pt[b, pages_per_blk * j + which], 0, 0, 0))

    grid_spec = pltpu.PrefetchScalarGridSpec(
        num_scalar_prefetch=1,
        grid=(n_dec, nbp),
        in_specs=[new_blk, new_blk, new_blk, page_spec(0), page_spec(1), page_spec(0), page_spec(1),
                  pl.BlockSpec((H_ATT, NEW_PAD, MOBA_BLOCK), lambda b, j, pt: (0, 0, 0)),
                  pl.BlockSpec((H_ATT, NEW_PAD, LANES), lambda b, j, pt: (0, 0, 0))],
        out_specs=new_blk,
        scratch_shapes=[pltpu.VMEM((nbp, H_ATT, NEW_PAD, MOBA_BLOCK), F32),
                        pltpu.VMEM((nbp, H_ATT, HEAD_DIM, MOBA_BLOCK), BF16),
                        pltpu.VMEM((H_ATT, NEW_PAD, LANES), F32)])
    return pl.pallas_call(
        kern,
        grid_spec=grid_spec,
        out_shape=jax.ShapeDtypeStruct((n_dec, H_ATT, NEW_PAD, HEAD_DIM), F32),
        compiler_params=_cparams(("arbitrary", "arbitrary")),
        name="att_sample",
    )(page_table, q_s, k_new, v_new, cache_kt, cache_kt, cache_vt, cache_vt, t_last, t_own)


def _mixp_kernel(u_ref, ws_ref, bs_ref, sg_ref, sb_ref, wdw_ref, bdw_ref, cg_ref, cb_ref,
                 sc_ref, z_ref, cbuf_ref, hist, *, tm):
    i = pl.program_id(1)
    last = pl.num_programs(1) - 1
    u = jax.nn.gelu(u_ref[:, 0:D_SGU])
    z = _ln(jax.nn.gelu(u_ref[:, D_SGU:2 * D_SGU]), sg_ref[...], sb_ref[...])
    tri_r = lax.broadcasted_iota(I32, (CHUNK, CHUNK), 0)
    tri_c = lax.broadcasted_iota(I32, (CHUNK, CHUNK), 1)
    lane = lax.broadcasted_iota(I32, (CHUNK, PAIR), 1)
    w_tril = [jnp.where(tri_c <= tri_r, ws_ref[h], 0.0).astype(BF16) for h in range(H_SGU)]
    for c in range(tm // CHUNK):
        rows = slice(c * CHUNK, (c + 1) * CHUNK)
        for pp in range(H_SGU // 2):
            cols = slice(pp * PAIR, (pp + 1) * PAIR)
            z2 = z[rows, cols].astype(BF16)
            s0 = jnp.dot(w_tril[2 * pp], z2, preferred_element_type=F32)
            s1 = jnp.dot(w_tril[2 * pp + 1], z2, preferred_element_type=F32)
            s2 = jnp.where(lane < HEAD_DIM, s0, s1) + bs_ref[:, cols]
            sc_ref[rows, cols] = (u[rows, cols] * s2).astype(sc_ref.dtype)

    @pl.when(i == last)
    def _():
        z_ref[0] = z[tm - CHUNK:tm]

    glu = u_ref[:, 2 * D_SGU:2 * D_SGU + D_CONV] * jax.nn.sigmoid(u_ref[:, 2 * D_SGU + D_CONV:D_REST])

    @pl.when(i == 0)
    def _():
        hist[0:HALO, :] = jnp.zeros((HALO, D_CONV), F32)

    hist[HALO:HALO + tm, :] = glu
    y = jnp.zeros((tm, D_CONV), F32) + bdw_ref[...]
    for w in range(CONV_W):
        y = y + hist[pl.ds(HALO - (CONV_W - 1) + w, tm), :] * wdw_ref[pl.ds(w, 1), :]
    sc_ref[:, D_SGU:D_SGU + D_CONV] = jax.nn.silu(_ln(y, cg_ref[...], cb_ref[...])).astype(sc_ref.dtype)

    @pl.when(i == last)
    def _():
        cbuf_ref[0] = hist[pl.ds(HALO + tm - (CONV_W - 1), CONV_W - 1), :]

    hist[0:HALO, :] = hist[pl.ds(tm, HALO), :]


def _mix_prompt(u_all, lw, n_batch, seq, tm):
    nt = seq // tm
    kern = functools.partial(_mixp_kernel, tm=tm)
    full = lambda shape: pl.BlockSpec(shape, lambda b, i: (0,) * len(shape))
    return pl.pallas_call(
        kern,
        grid=(n_batch, nt),
        in_specs=[pl.BlockSpec((tm, D_REST), lambda b, i: (b * nt + i, 0)),
                  full((H_SGU, CHUNK, CHUNK)), full((CHUNK, D_SGU)), full((1, D_SGU)), full((1, D_SGU)),
                  full((CONV_W, D_CONV)), full((1, D_CONV)), full((1, D_CONV)), full((1, D_CONV))],
        out_specs=[pl.BlockSpec((tm, D_SGU + D_CONV), lambda b, i: (b * nt + i, 0)),
                   pl.BlockSpec((1, CHUNK, D_SGU), lambda b, i: (b, 0, 0)),
                   pl.BlockSpec((1, CONV_W - 1, D_CONV), lambda b, i: (b, 0, 0))],
        out_shape=[jax.ShapeDtypeStruct((n_batch * seq, D_SGU + D_CONV), BF16),
                   jax.ShapeDtypeStruct((n_batch, CHUNK, D_SGU), F32),
                   jax.ShapeDtypeStruct((n_batch, CONV_W - 1, D_CONV), F32)],
        scratch_shapes=[pltpu.VMEM((HALO + tm, D_CONV), F32)],
        compiler_params=_cparams(("arbitrary", "arbitrary")),
        name="mix_prompt",
    )(u_all, lw["w_s"], lw["bs_full"], lw["sgu_g"], lw["sgu_b"], lw["w_dw"], lw["b_dw"],
      lw["cln_g"], lw["cln_b"])


def _mixs_kernel(u_ref, st_ref, wv_ref, bs_ref, sg_ref, sb_ref, wsh_ref, wdw_ref, bdw_ref, cg_ref, cb_ref,
                 sc_ref, z_ref, glu_ref, *, n_dec, n_new):
    u = jax.nn.gelu(u_ref[:, 0:D_SGU])
    z = _ln(jax.nn.gelu(u_ref[:, D_SGU:2 * D_SGU]), sg_ref[...], sb_ref[...])
    z_ref[...] = z
    glu = u_ref[:, 2 * D_SGU:2 * D_SGU + D_CONV] * jax.nn.sigmoid(u_ref[:, 2 * D_SGU + D_CONV:D_REST])
    glu_ref[...] = glu
    st = st_ref[...]
    for i in range(n_new):
        rows = slice(i * n_dec, (i + 1) * n_dec)
        s = jnp.zeros((n_dec, D_SGU), F32) + bs_ref[pl.ds(i, 1), :]
        y = jnp.sum(st * wsh_ref[i][None], axis=1) + bdw_ref[...]
        for j in range(i + 1):
            src = slice(j * n_dec, (j + 1) * n_dec)
            s = s + wv_ref[pl.ds(i * n_new + j, 1), :] * z[src]
            y = y + glu[src] * wdw_ref[pl.ds(CONV_W - 1 - i + j, 1), :]
        sc_ref[rows, 0:D_SGU] = (u[rows] * s).astype(sc_ref.dtype)
        sc_ref[rows, D_SGU:D_SGU + D_CONV] = jax.nn.silu(_ln(y, cg_ref[...], cb_ref[...])).astype(sc_ref.dtype)


def _mix_sample(u_s, state, lw, n_dec, n_new):
    r = n_dec * n_new
    kern = functools.partial(_mixs_kernel, n_dec=n_dec, n_new=n_new)
    return pl.pallas_call(
        kern,
        out_shape=[jax.ShapeDtypeStruct((r, D_SGU + D_CONV), BF16),
                   jax.ShapeDtypeStruct((r, D_SGU), F32),
                   jax.ShapeDtypeStruct((r, D_CONV), F32)],
        compiler_params=pltpu.CompilerParams(vmem_limit_bytes=VMEM_LIMIT),
        name="mix_sample",
    )(u_s, state, lw["wv_s"], lw["bs_s"], lw["sgu_g"], lw["sgu_b"], lw["wsh"], lw["w_dw"], lw["b_dw"],
      lw["cln_g"], lw["cln_b"])


def _outproj_kernel(att_ref, sc_ref, x_ref, wo_ref, g_ref, b_ref, wrh_ref, wrl_ref, br_ref,
                    x1_ref, eidx_ref, rank_ref, gw_ref, cnt_ref, carry, *, tm, alpha):
    i = pl.program_id(0)
    mix = (jnp.dot(att_ref[...], wo_ref[0:D_ATT, :], preferred_element_type=F32)
           + jnp.dot(sc_ref[...], wo_ref[D_ATT:D_MODEL, :], preferred_element_type=F32))
    x1 = _ln(alpha * x_ref[...] + mix, g_ref[...], b_ref[...])
    x1_ref[...] = x1
    hi = x1.astype(BF16)
    lo = (x1 - hi.astype(F32)).astype(BF16)
    logits = (lax.dot_general(wrh_ref[...], hi, _NT, preferred_element_type=F32)
              + lax.dot_general(wrl_ref[...], hi, _NT, preferred_element_type=F32)
              + lax.dot_general(wrh_ref[...], lo, _NT, preferred_element_type=F32))
    scores = jax.nn.sigmoid(logits)
    biased = scores + br_ref[...]
    eio = lax.broadcasted_iota(I32, (GROUP_SIZE, tm), 0)
    big = 1 << 20
    sg = [scores[g * GROUP_SIZE:(g + 1) * GROUP_SIZE] for g in range(N_GROUPS)]
    bg = [biased[g * GROUP_SIZE:(g + 1) * GROUP_SIZE] for g in range(N_GROUPS)]
    gio = lax.broadcasted_iota(I32, (N_GROUPS, tm), 0)
    gscore = jnp.zeros((N_GROUPS, tm), F32)
    for g in range(N_GROUPS):
        m1 = jnp.max(bg[g], axis=0, keepdims=True)
        i1 = jnp.min(jnp.where(bg[g] == m1, eio, big), axis=0, keepdims=True)
        m2 = jnp.max(jnp.where(eio == i1, -jnp.inf, bg[g]), axis=0, keepdims=True)
        gscore = jnp.where(gio == g, m1 + m2, gscore)
    gsel = _select_topk_sublane(gscore, gio >= 0, gio, TOPK_GROUPS)
    mg = [jnp.where(gsel[g:g + 1] > 0.0, bg[g], -jnp.inf) for g in range(N_GROUPS)]
    sio = lax.broadcasted_iota(I32, (TOP_K, tm), 0)
    eidx = jnp.zeros((TOP_K, tm), I32)
    gws = jnp.zeros((TOP_K, tm), F32)
    selm = [jnp.zeros((GROUP_SIZE, tm), F32) for _ in range(N_GROUPS)]
    for s in range(TOP_K):
        m = jnp.max(mg[0], axis=0, keepdims=True)
        for g in range(1, N_GROUPS):
            m = jnp.maximum(m, jnp.max(mg[g], axis=0, keepdims=True))
        idx = jnp.full((1, tm), big, I32)
        for g in range(N_GROUPS):
            idx = jnp.minimum(idx, jnp.min(jnp.where(mg[g] == m, eio + g * GROUP_SIZE, big),
                                           axis=0, keepdims=True))
        sc = jnp.zeros((1, tm), F32)
        for g in range(N_GROUPS):
            pick = (eio + g * GROUP_SIZE) == idx
            sc = sc + jnp.sum(jnp.where(pick, sg[g], 0.0), axis=0, keepdims=True)
            mg[g] = jnp.where(pick, -jnp.inf, mg[g])
            selm[g] = jnp.where(pick, 1.0, selm[g])
        eidx = jnp.where(sio == s, idx, eidx)
        gws = jnp.where(sio == s, sc, gws)
    gw_ref[...] = gws / jnp.sum(gws, axis=0, keepdims=True) * ROUTE_SCALE
    eidx_ref[...] = eidx

    @pl.when(i == 0)
    def _():
        carry[...] = jnp.zeros(carry.shape, F32)

    sel_all = jnp.concatenate(selm, axis=0)
    t_r = lax.broadcasted_iota(I32, (tm, tm), 0)
    t_c = lax.broadcasted_iota(I32, (tm, tm), 1)
    before = jnp.where(t_r < t_c, 1.0, 0.0).astype(BF16)
    prior = jnp.dot(sel_all.astype(BF16), before, preferred_element_type=F32) + carry[:, 0:1]
    rank = jnp.zeros((TOP_K, tm), F32)
    for s in range(TOP_K):
        r_s = jnp.zeros((1, tm), F32)
        for g in range(N_GROUPS):
            pick = (eio + g * GROUP_SIZE) == eidx[s:s + 1]
            r_s = r_s + jnp.sum(jnp.where(pick, prior[g * GROUP_SIZE:(g + 1) * GROUP_SIZE], 0.0),
                                axis=0, keepdims=True)
        rank = jnp.where(sio == s, r_s, rank)
    rank_ref[...] = rank.astype(I32)
    carry[...] = carry[...] + jnp.sum(sel_all, axis=1, keepdims=True)
    cnt_ref[...] = carry[...]


def _outproj(att, sc, x, lw, tm, alpha):
    r = x.shape[0]
    kern = functools.partial(_outproj_kernel, tm=tm, alpha=alpha)
    row_blk = lambda w: pl.BlockSpec((tm, w), lambda i: (i, 0))
    col_blk = pl.BlockSpec((TOP_K, tm), lambda i: (0, i))
    full = lambda shape: pl.BlockSpec(shape, lambda i: (0,) * len(shape))
    return pl.pallas_call(
        kern,
        grid=(r // tm,),
        in_specs=[row_blk(D_ATT), row_blk(D_SGU + D_CONV), row_blk(D_MODEL), full((D_MODEL, D_MODEL)),
                  full((1, D_MODEL)), full((1, D_MODEL)), full((N_EXPERTS, D_MODEL)),
                  full((N_EXPERTS, D_MODEL)), full((N_EXPERTS, 1))],
        out_specs=[row_blk(D_MODEL), col_blk, col_blk, col_blk, full((N_EXPERTS, LANES))],
        out_shape=[jax.ShapeDtypeStruct((r, D_MODEL), F32),
                   jax.ShapeDtypeStruct((TOP_K, r), I32),
                   jax.ShapeDtypeStruct((TOP_K, r), I32),
                   jax.ShapeDtypeStruct((TOP_K, r), F32),
                   jax.ShapeDtypeStruct((N_EXPERTS, LANES), F32)],
        scratch_shapes=[pltpu.VMEM((N_EXPERTS, LANES), F32)],
        compiler_params=_cparams(("arbitrary",)),
        name="outproj_router",
    )(att, sc, x, lw["w_out"], lw["ln1_g"], lw["ln1_b"], lw["wr_hi"], lw["wr_lo"], lw["b_router"])


def _dispatch_kernel(dest_hbm, x_ref, xs_in, xs_out, idx, sem_idx, sem):
    del xs_in
    i = pl.program_id(0)
    cp = pltpu.make_async_copy(dest_hbm.at[i], idx, sem_idx)
    cp.start()
    cp.wait()

    def start(t, c):
        for s in range(TOP_K):
            pltpu.make_async_copy(x_ref.at[pl.ds(t, 1)], xs_out.at[pl.ds(idx[t * TOP_K + s], 1)],
                                  sem).start(priority=s % 2)
        return c

    lax.fori_loop(0, TOK_TILE, start, 0)
    for _ in range(TOP_K):
        pltpu.make_async_copy(x_ref, xs_out.at[pl.ds(0, TOK_TILE)], sem).wait()


def _dispatch(dest_tiles, x1, xs_buf):
    r = x1.shape[0]
    return pl.pallas_call(
        _dispatch_kernel,
        grid=(r // TOK_TILE,),
        in_specs=[pl.BlockSpec(memory_space=pl.ANY),
                  pl.BlockSpec((TOK_TILE, D_MODEL), lambda i: (i, 0)),
                  pl.BlockSpec(memory_space=pl.ANY)],
        out_specs=pl.BlockSpec(memory_space=pl.ANY),
        out_shape=jax.ShapeDtypeStruct(xs_buf.shape, xs_buf.dtype),
        scratch_shapes=[pltpu.SMEM((TOK_TILE * TOP_K,), I32),
                        pltpu.SemaphoreType.DMA(()), pltpu.SemaphoreType.DMA(())],
        input_output_aliases={2: 0},
        compiler_params=_cparams(("arbitrary",)),
        name="dispatch",
    )(dest_tiles, x1, xs_buf)


def _experts_kernel(be_ref, nu_ref, xs_ref, wgu_ref, wdn_ref, ys_ref, wgu_b, wdn_b):
    i = pl.program_id(0)
    prev = be_ref[jnp.maximum(i - 1, 0)]
    fresh = (i == 0) | (be_ref[i] != prev)

    @pl.when(fresh)
    def _():
        wgu_b[...] = wgu_ref[0, 0].astype(BF16)
        wdn_b[...] = wdn_ref[0, 0].astype(BF16)

    @pl.when(i < nu_ref[0])
    def _():
        h = jnp.dot(xs_ref[...].astype(BF16), wgu_b[...], preferred_element_type=F32)
        act = jax.nn.silu(h[:, 0:D_EXPERT]) * h[:, D_EXPERT:2 * D_EXPERT]
        ys_ref[...] = jnp.dot(act.astype(BF16), wdn_b[...], preferred_element_type=F32)

    @pl.when(i >= nu_ref[0])
    def _():
        ys_ref[...] = jnp.zeros(ys_ref.shape, F32)


def _experts(blk_e, n_used, xs, w_gu, w_dn, layer):
    rows = xs.shape[0]
    grid_spec = pltpu.PrefetchScalarGridSpec(
        num_scalar_prefetch=2,
        grid=(rows // EXPERT_ROWS,),
        in_specs=[pl.BlockSpec((EXPERT_ROWS, D_MODEL), lambda i, be, nu: (i, 0)),
                  pl.BlockSpec((1, 1, D_MODEL, 2 * D_EXPERT), lambda i, be, nu: (layer, be[i], 0, 0)),
                  pl.BlockSpec((1, 1, D_EXPERT, D_MODEL), lambda i, be, nu: (layer, be[i], 0, 0))],
        out_specs=pl.BlockSpec((EXPERT_ROWS, D_MODEL), lambda i, be, nu: (i, 0)),
        scratch_shapes=[pltpu.VMEM((D_MODEL, 2 * D_EXPERT), BF16), pltpu.VMEM((D_EXPERT, D_MODEL), BF16)])
    return pl.pallas_call(
        _experts_kernel,
        grid_spec=grid_spec,
        out_shape=jax.ShapeDtypeStruct((rows, D_MODEL), F32),
        compiler_params=_cparams(("arbitrary",)),
        name="experts",
    )(blk_e, n_used, xs, w_gu, w_dn)


def _combine_kernel(dest_hbm, gw_ref, x1_ref, ys_hbm, wsg_ref, wsd_ref, g_ref, b_ref, x2_ref,
                    idx, buf, sem_idx, sem, *, alpha):
    i = pl.program_id(0)
    cp = pltpu.make_async_copy(dest_hbm.at[i], idx, sem_idx)
    cp.start()
    cp.wait()

    def start(t, c):
        for s in range(TOP_K):
            pltpu.make_async_copy(ys_hbm.at[pl.ds(idx[t * TOP_K + s], 1)], buf.at[s, pl.ds(t, 1)],
                                  sem).start(priority=s % 2)
        return c

    lax.fori_loop(0, TOK_TILE, start, 0)
    x1 = x1_ref[...]
    h = jnp.dot(x1.astype(BF16), wsg_ref[...], preferred_element_type=F32)
    act = jax.nn.silu(h[:, 0:D_EXPERT]) * h[:, D_EXPERT:2 * D_EXPERT]
    f = jnp.dot(act.astype(BF16), wsd_ref[...], preferred_element_type=F32)
    for s in range(TOP_K):
        pltpu.make_async_copy(ys_hbm.at[pl.ds(0, TOK_TILE)], buf.at[s], sem).wait()
    gw = gw_ref[...]
    for s in range(TOP_K):
        f = f + gw[:, s:s + 1] * buf[s]
    x2_ref[...] = _ln(alpha * x1 + f, g_ref[...], b_ref[...])


def _combine(dest_tiles, gw_tok, x1, ys, lw, alpha):
    r = x1.shape[0]
    kern = functools.partial(_combine_kernel, alpha=alpha)
    full = lambda shape: pl.BlockSpec(shape, lambda i: (0,) * len(shape))
    return pl.pallas_call(
        kern,
        grid=(r // TOK_TILE,),
        in_specs=[pl.BlockSpec(memory_space=pl.ANY),
                  pl.BlockSpec((TOK_TILE, TOP_K), lambda i: (i, 0)),
                  pl.BlockSpec((TOK_TILE, D_MODEL), lambda i: (i, 0)),
                  pl.BlockSpec(memory_space=pl.ANY),
                  full((D_MODEL, 2 * D_EXPERT)), full((D_EXPERT, D_MODEL)),
                  full((1, D_MODEL)), full((1, D_MODEL))],
        out_specs=pl.BlockSpec((TOK_TILE, D_MODEL), lambda i: (i, 0)),
        out_shape=jax.ShapeDtypeStruct((r, D_MODEL), F32),
        scratch_shapes=[pltpu.SMEM((TOK_TILE * TOP_K,), I32),
                        pltpu.VMEM((TOP_K, TOK_TILE, D_MODEL), F32),
                        pltpu.SemaphoreType.DMA(()), pltpu.SemaphoreType.DMA(())],
        compiler_params=_cparams(("arbitrary",)),
        name="combine",
    )(dest_tiles, gw_tok, x1, ys, lw["ws_gu"], lw["ws_dn"], lw["ln2_g"], lw["ln2_b"])


def _bias_tables(rel_bias, n_new, past_len):
    far = rel_bias[_t5_bucket(jnp.asarray(MOBA_BLOCK + 1, I32))]
    qi = jnp.arange(Q_BLOCK, dtype=I32)[:, None]
    ko = jnp.arange(MOBA_BLOCK, dtype=I32)[None, :]
    k2 = jnp.arange(2 * MOBA_BLOCK, dtype=I32)[None, :]
    tiles = []
    for first in range(2):
        for qoff in range(MOBA_BLOCK // Q_BLOCK):
            dist = qoff * Q_BLOCK + qi - k2 + (1 - first) * MOBA_BLOCK
            t = rel_bias[_t5_bucket(dist)] - far
            t = jnp.where((dist >= 0)[..., None], t, NEG)
            tiles.append(t.transpose(2, 0, 1))
    bt = jnp.stack(tiles, axis=1)
    bt = bt.reshape(N_PAIRS, 2, len(tiles), Q_BLOCK, 2 * MOBA_BLOCK)
    i_new = jnp.arange(NEW_PAD, dtype=I32)[:, None]
    last_start = (past_len // MOBA_BLOCK - 1) * MOBA_BLOCK
    dist_l = past_len + i_new - (last_start + ko)
    t_last = (rel_bias[_t5_bucket(dist_l)] - far).transpose(2, 0, 1)
    kn = jnp.arange(LANES, dtype=I32)[None, :]
    dist_o = i_new - kn
    t_own = (rel_bias[_t5_bucket(dist_o)] - far).transpose(2, 0, 1)
    t_own = jnp.where(((dist_o >= 0) & (kn < n_new))[None], t_own, NEG)
    return bt, t_last, t_own


def _layer_weights(l, w_in, w_out, w_s, b_s, sgu_ln_g, sgu_ln_b, w_dw, b_dw, conv_ln_g, conv_ln_b,
                   ln1_g, ln1_b, w_router, b_router, w_sh_gu, w_sh_dn, ln2_g, ln2_b, n_new):
    row = lambda a: a[l][None, :]
    ws = w_s[l]
    tril = jnp.tril(ws[:, :n_new, :n_new])
    wv_s = jnp.repeat(tril.transpose(1, 2, 0), HEAD_DIM, axis=-1).reshape(n_new * n_new, D_SGU)
    bs_s = jnp.repeat(b_s[l][:, :n_new].T, HEAD_DIM, axis=-1)
    bs_full = jnp.repeat(b_s[l].T, HEAD_DIM, axis=-1)
    r_io = jnp.arange(CONV_W - 1)[None, :]
    i_io = jnp.arange(n_new)[:, None]
    tap = r_io - i_io
    wsh = jnp.where((tap >= 0)[..., None], w_dw[l][jnp.clip(tap, 0, CONV_W - 1)], 0.0)
    wr = w_router[l].T
    wr_hi = wr.astype(BF16)
    return dict(
        w_in=w_in[l].astype(BF16), w_out=w_out[l].astype(BF16), w_s=ws, bs_full=bs_full,
        sgu_g=row(sgu_ln_g), sgu_b=row(sgu_ln_b), w_dw=w_dw[l], b_dw=row(b_dw),
        cln_g=row(conv_ln_g), cln_b=row(conv_ln_b), wv_s=wv_s, bs_s=bs_s, wsh=wsh,
        ln1_g=row(ln1_g), ln1_b=row(ln1_b), wr_hi=wr_hi, wr_lo=(wr - wr_hi.astype(F32)).astype(BF16),
        b_router=b_router[l][:, None], ws_gu=w_sh_gu[l].astype(BF16), ws_dn=w_sh_dn[l].astype(BF16),
        ln2_g=row(ln2_g), ln2_b=row(ln2_b))


@jax.jit
def _forward(x_prompt, x_sample, cache_k, cache_v, page_table, state_conv, rel_bias, w_in, w_out,
             w_s, b_s, sgu_ln_g, sgu_ln_b, w_dw, b_dw, conv_ln_g, conv_ln_b, ln1_g, ln1_b,
             w_router, b_router, w_exp_gu, w_exp_dn, w_sh_gu, w_sh_dn, ln2_g, ln2_b):
    depth = w_in.shape[0]
    n_batch, seq, _ = x_prompt.shape
    n_dec, n_new, _ = x_sample.shape
    n_pool = cache_k.shape[1]
    past_len = page_table.shape[1] * PAGE_SIZE
    alpha = (2 * depth) ** 0.25
    rp = n_batch * seq
    rs = n_dec * n_new
    r = rp + rs
    tm = 512 if (seq % 512 == 0 and rs % 512 == 0) else MOBA_BLOCK
    assert seq % tm == 0 and rs % tm == 0 and seq % MOBA_BLOCK == 0 and seq // MOBA_BLOCK <= HEAD_DIM
    assert (seq // MOBA_BLOCK) % 8 == 0 and past_len % MOBA_BLOCK == 0 and n_dec % 8 == 0
    assert MAX_DISTANCE <= MOBA_BLOCK and r % TOK_TILE == 0
    assert (seq // MOBA_BLOCK) % FAR_GROUP == 0 and seq >= 2 * MOBA_BLOCK
    del n_pool

    n_assign = r * TOP_K
    n_blk = (n_assign + N_EXPERTS * (EXPERT_ROWS - 1) + EXPERT_ROWS - 1) // EXPERT_ROWS
    rows_pad = n_blk * EXPERT_ROWS

    bt, t_last, t_own = _bias_tables(rel_bias, n_new, past_len)
    cache_kt = cache_k.transpose(0, 1, 3, 4, 2)
    cache_vt = cache_v.transpose(0, 1, 3, 4, 2)

    def to_heads(a):
        a = a.reshape(n_new, n_dec, H_ATT, HEAD_DIM).transpose(1, 2, 0, 3)
        return jnp.pad(a, ((0, 0), (0, 0), (0, NEW_PAD - n_new), (0, 0)))

    x = jnp.concatenate([x_prompt.reshape(rp, D_MODEL),
                         x_sample.transpose(1, 0, 2).reshape(rs, D_MODEL)], axis=0)
    xs_buf = jnp.zeros((rows_pad, D_MODEL), F32)

    def to_bi(a):
        return a.reshape(n_new, n_dec, a.shape[-1]).transpose(1, 0, 2)

    outs = [[] for _ in range(8)]
    for l in range(depth):
        lw = _layer_weights(l, w_in, w_out, w_s, b_s, sgu_ln_g, sgu_ln_b, w_dw, b_dw, conv_ln_g,
                            conv_ln_b, ln1_g, ln1_b, w_router, b_router, w_sh_gu, w_sh_dn,
                            ln2_g, ln2_b, n_new)
        qb, kf, vf, ka, kb, vb, u_all, kmean = _inproj(x, lw["w_in"], tm, seq)
        kmean_p = kmean[:rp // MOBA_BLOCK].reshape(n_batch, seq // MOBA_BLOCK, D_ATT)
        att_p = _att_prompt(qb, ka, kb, vb, kmean_p, bt, n_batch, seq)
        k_s = to_bi(kf[rp:])
        v_s = to_bi(vf[rp:])
        att_s = _att_sample(page_table, to_heads(qb[rp:].astype(F32)), to_heads(kf[rp:]), to_heads(vf[rp:]),
                            cache_kt, cache_vt, l, t_last, t_own, n_new)
        att_s = att_s[:, :, :n_new].transpose(2, 0, 1, 3).reshape(rs, D_ATT).astype(BF16)
        sc_p, z_last, cbuf_p = _mix_prompt(u_all, lw, n_batch, seq, tm)
        sc_s, z_s, glu_s = _mix_sample(u_all[rp:], state_conv[l], lw, n_dec, n_new)
        att = jnp.concatenate([att_p, att_s], axis=0)
        sc = jnp.concatenate([sc_p, sc_s], axis=0)
        x1, eidx, rank, gw, cnt = _outproj(att, sc, x, lw, tm, alpha)
        counts = cnt[:, 0].astype(I32)
        padded = (counts + EXPERT_ROWS - 1) // EXPERT_ROWS * EXPERT_ROWS
        pend = jnp.cumsum(padded)
        pstart = pend - padded
        e_io = jnp.arange(N_EXPERTS, dtype=I32)
        pstart_of = jnp.sum(jnp.where(eidx[..., None] == e_io, pstart, 0), axis=-1)
        dest = (pstart_of + rank).T.reshape(r // TOK_TILE, TOK_TILE * TOP_K)
        blk_rows = jnp.arange(n_blk, dtype=I32)[:, None] * EXPERT_ROWS
        blk_e = jnp.minimum(jnp.sum((pend[None, :] <= blk_rows).astype(I32), axis=1), N_EXPERTS - 1)
        n_used = (pend[-1:] // EXPERT_ROWS).astype(I32)
        xs_buf = _dispatch(dest, x1, xs_buf)
        ys = _experts(blk_e, n_used, xs_buf, w_exp_gu, w_exp_dn, l)
        x = _combine(dest, gw.T, x1, ys, lw, alpha)

        outs[0].append(kf[:rp].reshape(n_batch, seq, H_ATT, HEAD_DIM))
        outs[1].append(vf[:rp].reshape(n_batch, seq, H_ATT, HEAD_DIM))
        outs[2].append(k_s.reshape(n_dec, n_new, H_ATT, HEAD_DIM))
        outs[3].append(v_s.reshape(n_dec, n_new, H_ATT, HEAD_DIM))
        outs[4].append(z_last)
        outs[5].append(to_bi(z_s))
        outs[6].append(cbuf_p)
        outs[7].append(jnp.concatenate([state_conv[l], to_bi(glu_s)], axis=1)[:, -(CONV_W - 1):])
    y_prompt = x[:rp].reshape(n_batch, seq, D_MODEL)
    y_sample = to_bi(x[rp:])
    return (y_prompt, y_sample) + tuple(jnp.stack(o) for o in outs)


def kernel(x_prompt, x_sample, cache_k, cache_v, page_table, state_conv, rel_bias, w_in, w_out, w_s, b_s,
           sgu_ln_g, sgu_ln_b, w_dw, b_dw, conv_ln_g, conv_ln_b, ln1_g, ln1_b, w_router, b_router,
           w_exp_gu, w_exp_dn, w_sh_gu, w_sh_dn, ln2_g, ln2_b):
    return _forward(x_prompt, x_sample, cache_k, cache_v, page_table, state_conv, rel_bias, w_in, w_out,
                    w_s, b_s, sgu_ln_g, sgu_ln_b, w_dw, b_dw, conv_ln_g, conv_ln_b, ln1_g, ln1_b,
                    w_router, b_router, w_exp_gu, w_exp_dn, w_sh_gu, w_sh_dn, ln2_g, ln2_b)
```

```python
import functools
import math

import jax
import jax.numpy as jnp
from jax import lax
from jax.experimental import pallas as pl
from jax.experimental.pallas import tpu as pltpu

F32 = jnp.float32
BF16 = jnp.bfloat16
I32 = jnp.int32

D_MODEL = 1024
HEAD_DIM = 64
H_ATT = 8
H_SGU = 4
D_ATT = H_ATT * HEAD_DIM
D_SGU = H_SGU * HEAD_DIM
D_CONV = D_MODEL - D_ATT - D_SGU
D_IN = 3 * D_ATT + 2 * D_SGU + 2 * D_CONV
D_REST = 2 * D_SGU + 2 * D_CONV
CHUNK = 128
CONV_W = 31
MOBA_BLOCK = 256
MOBA_TOPK = 3
Q_BLOCK = 256
NUM_BUCKETS = 32
MAX_DISTANCE = 128
N_EXPERTS = 64
TOP_K = 8
N_GROUPS = 8
GROUP_SIZE = N_EXPERTS // N_GROUPS
TOPK_GROUPS = 4
D_EXPERT = 256
ROUTE_SCALE = 2.5
LN_EPS = 1e-5
PAGE_SIZE = 128

LANES = 128
PAIR = 2 * HEAD_DIM
N_PAIRS = H_ATT // 2
NEG = -1e30
FAR_GROUP = 4
NEW_PAD = 8
SAMPLE_BLOCKS_PER_STEP = 4
EXPERT_ROWS = 256
TOK_TILE = 128
HALO = 32
VMEM_LIMIT = 56 * 1024 * 1024

_NT = (((1,), (1,)), ((), ()))


def _cparams(sem):
    return pltpu.CompilerParams(dimension_semantics=sem, vmem_limit_bytes=VMEM_LIMIT)


def _ln(x, g, b):
    mu = jnp.mean(x, axis=-1, keepdims=True)
    xc = x - mu
    var = jnp.mean(xc * xc, axis=-1, keepdims=True)
    return xc * lax.rsqrt(var + LN_EPS) * g + b


def _t5_bucket(dist):
    n = jnp.maximum(dist, 0)
    max_exact = NUM_BUCKETS // 2
    nf = jnp.maximum(n, 1).astype(F32)
    large = max_exact + (jnp.log(nf / max_exact) / math.log(MAX_DISTANCE / max_exact)
                         * (NUM_BUCKETS - max_exact)).astype(I32)
    large = jnp.minimum(large, NUM_BUCKETS - 1)
    return jnp.where(n < max_exact, n, large)


def _inproj_kernel(x_ref, w_ref, q_ref, k_ref, v_ref, ka_ref, kb_ref, vb_ref, u_ref, km_ref,
                   *, tm, tiles_per_seq):
    i = pl.program_id(0)
    xb = x_ref[...].astype(BF16)

    def seg(lo, hi):
        return jnp.dot(xb, w_ref[:, lo:hi], preferred_element_type=F32)

    q = seg(0, D_ATT)
    q_ref[...] = (q * (HEAD_DIM ** -0.5)).astype(BF16)
    k = seg(D_ATT, 2 * D_ATT)
    k_ref[...] = k
    v = seg(2 * D_ATT, 3 * D_ATT)
    v_ref[...] = v
    vb_ref[...] = v.astype(BF16)
    u_ref[...] = seg(3 * D_ATT, D_IN)
    for c in range(tm // MOBA_BLOCK):
        km_ref[c] = jnp.mean(k[c * MOBA_BLOCK:(c + 1) * MOBA_BLOCK], axis=0, keepdims=True)
    row = lax.broadcasted_iota(I32, (tm, D_ATT), 0)
    col = lax.broadcasted_iota(I32, (tm, D_ATT), 1)
    blk = lax.rem(i, tiles_per_seq) * (tm // MOBA_BLOCK) + row // MOBA_BLOCK
    within = col % PAIR
    ka_ref[...] = jnp.where(within < HEAD_DIM, k, jnp.where(within - HEAD_DIM == blk, 1.0, 0.0)).astype(BF16)
    kb_ref[...] = jnp.where(within >= HEAD_DIM, k, jnp.where(within == blk, 1.0, 0.0)).astype(BF16)


def _inproj(x, w_in_b, tm, seq):
    r = x.shape[0]
    n = r // tm
    kern = functools.partial(_inproj_kernel, tm=tm, tiles_per_seq=max(seq // tm, 1))
    row_blk = lambda w: pl.BlockSpec((tm, w), lambda i: (i, 0))
    return pl.pallas_call(
        kern,
        grid=(n,),
        in_specs=[row_blk(D_MODEL), pl.BlockSpec((D_MODEL, D_IN), lambda i: (0, 0))],
        out_specs=[row_blk(D_ATT)] * 6 + [row_blk(D_REST),
                   pl.BlockSpec((tm // MOBA_BLOCK, 1, D_ATT), lambda i: (i, 0, 0))],
        out_shape=[jax.ShapeDtypeStruct((r, D_ATT), BF16),
                   jax.ShapeDtypeStruct((r, D_ATT), F32),
                   jax.ShapeDtypeStruct((r, D_ATT), F32),
                   jax.ShapeDtypeStruct((r, D_ATT), BF16),
                   jax.ShapeDtypeStruct((r, D_ATT), BF16),
                   jax.ShapeDtypeStruct((r, D_ATT), BF16),
                   jax.ShapeDtypeStruct((r, D_REST), F32),
                   jax.ShapeDtypeStruct((r // MOBA_BLOCK, 1, D_ATT), F32)],
        compiler_params=_cparams(("arbitrary",)),
        name="inproj",
    )(x, w_in_b)


def _select_topk_sublane(g, valid, blk, k_sel):
    sel = jnp.zeros(g.shape, F32)
    g = jnp.where(valid, g, -jnp.inf)
    for _ in range(k_sel):
        m = jnp.max(g, axis=0, keepdims=True)
        cand = jnp.where((g == m) & valid & (sel == 0.0), blk, 1 << 20)
        idx = jnp.min(cand, axis=0, keepdims=True)
        pick = cand == idx
        pick = pick & (idx < (1 << 20))
        sel = jnp.where(pick, 1.0, sel)
        g = jnp.where(pick, -jnp.inf, g)
    return sel


def _attp_kernel(q_ref, ka_ref, kb_ref, v_ref, km_ref, bt_ref, o_ref, s_a, s_b, *, nb):
    qi = pl.program_id(2)
    q_per_blk = MOBA_BLOCK // Q_BLOCK
    own = qi // q_per_blk
    qoff = lax.rem(qi, q_per_blk)
    q2 = q_ref[...]
    lane = lax.broadcasted_iota(I32, (Q_BLOCK, PAIR), 1)
    km = km_ref[0]
    lane_k = lax.broadcasted_iota(I32, (nb, PAIR), 1)
    blk = lax.broadcasted_iota(I32, (nb, Q_BLOCK), 0)
    sels = []
    for h in range(2):
        in_head = (lane_k < HEAD_DIM) if h == 0 else (lane_k >= HEAD_DIM)
        kmh = jnp.where(in_head, km, 0.0)
        hi = kmh.astype(BF16)
        lo = (kmh - hi.astype(F32)).astype(BF16)
        g = (lax.dot_general(hi, q2, _NT, preferred_element_type=F32)
             + lax.dot_general(lo, q2, _NT, preferred_element_type=F32))
        sels.append(_select_topk_sublane(g, blk < own, blk, min(MOBA_TOPK, nb)))
    zpad = jnp.zeros((HEAD_DIM - nb, Q_BLOCK), F32)
    q2f = q2.astype(F32)

    def augment(keep):
        pens = [jnp.where(keep(sels[h]), 0.0, NEG) for h in range(2)]
        pen = jnp.concatenate([pens[1], zpad, pens[0], zpad], axis=0).T
        return (jnp.where(lane < HEAD_DIM, q2f, pen).astype(BF16),
                jnp.where(lane >= HEAD_DIM, q2f, pen).astype(BF16))

    qa_near = augment(lambda sel: (sel > 0.0) | (blk == own))
    qa_far = augment(lambda sel: (sel > 0.0) & (blk < own - 1))
    krefs = (ka_ref, kb_ref)

    def rows(j, n):
        return pl.ds(pl.multiple_of(j * MOBA_BLOCK, MOBA_BLOCK), n * MOBA_BLOCK)

    first = jnp.where(own == 0, 1, 0)
    near = rows(jnp.maximum(own - 1, 0), 2)
    vt = v_ref[near, :]
    state = []
    for h in range(2):
        s = lax.dot_general(qa_near[h], krefs[h][near, :], _NT, preferred_element_type=F32)
        s = s + bt_ref[0, h, pl.ds(first * q_per_blk + qoff, 1)][0]
        m = jnp.max(s, axis=1, keepdims=True)
        p = jnp.exp(s - m)
        state.append((m, jnp.sum(p, axis=1, keepdims=True),
                      jnp.dot(p.astype(BF16), vt, preferred_element_type=F32)))

    n_groups = nb // FAR_GROUP

    def far_scores(g, buf):
        slab = rows(g * FAR_GROUP, FAR_GROUP)
        for h in range(2):
            buf[h] = lax.dot_general(qa_far[h], krefs[h][slab, :], _NT, preferred_element_type=F32)

    def far_update(g, buf, st):
        vt = v_ref[rows(g * FAR_GROUP, FAR_GROUP), :]
        new = []
        for h in range(2):
            m, l, acc = st[h]
            s = buf[h]
            m_new = jnp.maximum(m, jnp.max(s, axis=1, keepdims=True))
            alpha = jnp.exp(m - m_new)
            p = jnp.exp(s - m_new)
            new.append((m_new, alpha * l + jnp.sum(p, axis=1, keepdims=True),
                        alpha * acc + jnp.dot(p.astype(BF16), vt, preferred_element_type=F32)))
        return tuple(new)

    def far_body(i, st):
        g = 2 * i
        far_scores(g + 1, s_b)
        st = far_update(g, s_a, st)
        far_scores(jnp.minimum(g + 2, n_groups - 1), s_a)
        return far_update(g + 1, s_b, st)

    n_far = jnp.maximum(own - 1, 0)
    far_scores(0, s_a)
    st = lax.fori_loop(0, (n_far + 2 * FAR_GROUP - 1) // (2 * FAR_GROUP), far_body, tuple(state))
    o0 = st[0][2] / st[0][1]
    o1 = st[1][2] / st[1][1]
    o_ref[...] = jnp.where(lane < HEAD_DIM, o0, o1).astype(o_ref.dtype)


def _att_prompt(qb, ka, kb, vb, kmean, bias_tiles, n_batch, seq):
    nb = seq // MOBA_BLOCK
    nq = seq // Q_BLOCK
    kern = functools.partial(_attp_kernel, nb=nb)
    seq_blk = pl.BlockSpec((seq, PAIR), lambda b, p, i: (b, p))
    return pl.pallas_call(
        kern,
        grid=(n_batch, N_PAIRS, nq),
        in_specs=[pl.BlockSpec((Q_BLOCK, PAIR), lambda b, p, i: (b * nq + i, p)),
                  seq_blk, seq_blk, seq_blk,
                  pl.BlockSpec((1, nb, PAIR), lambda b, p, i: (b, 0, p)),
                  pl.BlockSpec((1, 2, 2 * (MOBA_BLOCK // Q_BLOCK), Q_BLOCK, 2 * MOBA_BLOCK),
                               lambda b, p, i: (p, 0, 0, 0, 0))],
        out_specs=pl.BlockSpec((Q_BLOCK, PAIR), lambda b, p, i: (b * nq + i, p)),
        out_shape=jax.ShapeDtypeStruct((n_batch * seq, D_ATT), BF16),
        scratch_shapes=[pltpu.VMEM((2, Q_BLOCK, FAR_GROUP * MOBA_BLOCK), F32)] * 2,
        compiler_params=_cparams(("arbitrary", "arbitrary", "arbitrary")),
        name="att_prompt",
    )(qb, ka, kb, vb, kmean, bias_tiles)


def _atts_kernel(pt_ref, q_ref, kn_ref, vn_ref, *refs, nbp, n_new, bps):
    del pt_ref
    pages = bps * (MOBA_BLOCK // PAGE_SIZE)
    k_refs, v_refs = refs[:pages], refs[pages:2 * pages]
    tl_ref, to_ref, o_ref, s_buf, v_buf, g_buf = refs[2 * pages:]
    j = pl.program_id(1)
    hqk = (((2,), (1,)), ((0,), (0,)))
    hpv = (((2,), (2,)), ((0,), (0,)))
    q = q_ref[0]
    qb = q.astype(BF16)
    lane = lax.broadcasted_iota(I32, (H_ATT, NEW_PAD, LANES), 2)

    @pl.when(j == 0)
    def _():
        g_buf[...] = jnp.full(g_buf.shape, -jnp.inf, F32)

    for c in range(bps):
        jb = j * bps + c
        kt = jnp.concatenate([k_refs[2 * c][0, 0], k_refs[2 * c + 1][0, 0]], axis=2).astype(BF16)
        s = lax.dot_general(qb, kt, hqk, preferred_element_type=F32)
        s_buf[jb] = s
        v_buf[jb] = jnp.concatenate([v_refs[2 * c][0, 0], v_refs[2 * c + 1][0, 0]], axis=2).astype(BF16)
        gj = jnp.sum(s, axis=2, keepdims=True)
        g_buf[...] = jnp.where(lane == jb, gj, g_buf[...])

    @pl.when(j == nbp // bps - 1)
    def _():
        g = g_buf[...]
        valid = lane < nbp
        sel = jnp.zeros(g.shape, F32)
        for _ in range(min(MOBA_TOPK, nbp + 1)):
            m = jnp.max(g, axis=2, keepdims=True)
            cand = jnp.where((g == m) & valid & (sel == 0.0), lane, 1 << 20)
            idx = jnp.min(cand, axis=2, keepdims=True)
            pick = (cand == idx) & (idx < (1 << 20))
            sel = jnp.where(pick, 1.0, sel)
            g = jnp.where(pick, -jnp.inf, g)
        pen = jnp.where(sel > 0.0, 0.0, NEG)
        kn = kn_ref[0]
        vn = vn_ref[0]
        t_own = to_ref[...]
        s_own = [jnp.sum(q * kn[:, i:i + 1, :], axis=2, keepdims=True) + t_own[:, :, i:i + 1]
                 for i in range(n_new)]
        m = s_own[0]
        for i in range(1, n_new):
            m = jnp.maximum(m, s_own[i])
        for jj in range(nbp):
            sj = s_buf[jj] + pen[:, :, jj:jj + 1]
            if jj == nbp - 1:
                sj = sj + tl_ref[...]
            s_buf[jj] = sj
            m = jnp.maximum(m, jnp.max(sj, axis=2, keepdims=True))
        l = jnp.zeros(m.shape, F32)
        acc = jnp.zeros(q.shape, F32)
        for i in range(n_new):
            p = jnp.exp(s_own[i] - m)
            l = l + p
            acc = acc + p * vn[:, i:i + 1, :]
        for jj in range(nbp):
            p = jnp.exp(s_buf[jj] - m)
            l = l + jnp.sum(p, axis=2, keepdims=True)
            acc = acc + lax.dot_general(p.astype(BF16), v_buf[jj], hpv, preferred_element_type=F32)
        o_ref[0] = acc / l


def _att_sample(page_table, q_s, k_new, v_new, cache_kt, cache_vt, layer, t_last, t_own, n_new):
    n_dec = q_s.shape[0]
    n_pages = page_table.shape[1]
    pages_per_blk = MOBA_BLOCK // PAGE_SIZE
    nbp = n_pages // pages_per_blk
    bps = math.gcd(nbp, SAMPLE_BLOCKS_PER_STEP)
    pages = bps * pages_per_blk
    kern = functools.partial(_atts_kernel, nbp=nbp, n_new=n_new, bps=bps)
    new_blk = pl.BlockSpec((1, H_ATT, NEW_PAD, HEAD_DIM), lambda b, j, pt: (b, 0, 0, 0))

    def page_spec(which):
        return pl.BlockSpec((1, 1, H_ATT, HEAD_DIM, PAGE_SIZE),
                            lambda b, j, pt: (layer, pt[b, pages * j + which], 0, 0, 0))

    page_specs = [page_spec(w) for w in range(pages)]
    grid_spec = pltpu.PrefetchScalarGridSpec(
        num_scalar_prefetch=1,
        grid=(n_dec, nbp // bps),
        in_specs=[new_blk, new_blk, new_blk] + page_specs + page_specs + [
                  pl.BlockSpec((H_ATT, NEW_PAD, MOBA_BLOCK), lambda b, j, pt: (0, 0, 0)),
                  pl.BlockSpec((H_ATT, NEW_PAD, LANES), lambda b, j, pt: (0, 0, 0))],
        out_specs=new_blk,
        scratch_shapes=[pltpu.VMEM((nbp, H_ATT, NEW_PAD, MOBA_BLOCK), F32),
                        pltpu.VMEM((nbp, H_ATT, HEAD_DIM, MOBA_BLOCK), BF16),
                        pltpu.VMEM((H_ATT, NEW_PAD, LANES), F32)])
    return pl.pallas_call(
        kern,
        grid_spec=grid_spec,
        out_shape=jax.ShapeDtypeStruct((n_dec, H_ATT, NEW_PAD, HEAD_DIM), F32),
        compiler_params=_cparams(("arbitrary", "arbitrary")),
        name="att_sample",
    )(page_table, q_s, k_new, v_new, *([cache_kt] * pages), *([cache_vt] * pages), t_last, t_own)


def _mixp_kernel(u_ref, ws_ref, bs_ref, sg_ref, sb_ref, wdw_ref, bdw_ref, cg_ref, cb_ref,
                 sc_ref, z_ref, cbuf_ref, hist, *, tm):
    i = pl.program_id(1)
    last = pl.num_programs(1) - 1
    u = jax.nn.gelu(u_ref[:, 0:D_SGU])
    z = _ln(jax.nn.gelu(u_ref[:, D_SGU:2 * D_SGU]), sg_ref[...], sb_ref[...])
    tri_r = lax.broadcasted_iota(I32, (CHUNK, CHUNK), 0)
    tri_c = lax.broadcasted_iota(I32, (CHUNK, CHUNK), 1)
    lane = lax.broadcasted_iota(I32, (CHUNK, PAIR), 1)
    w_tril = [jnp.where(tri_c <= tri_r, ws_ref[h], 0.0).astype(BF16) for h in range(H_SGU)]
    for c in range(tm // CHUNK):
        rows = slice(c * CHUNK, (c + 1) * CHUNK)
        for pp in range(H_SGU // 2):
            cols = slice(pp * PAIR, (pp + 1) * PAIR)
            z2 = z[rows, cols].astype(BF16)
            s0 = jnp.dot(w_tril[2 * pp], z2, preferred_element_type=F32)
            s1 = jnp.dot(w_tril[2 * pp + 1], z2, preferred_element_type=F32)
            s2 = jnp.where(lane < HEAD_DIM, s0, s1) + bs_ref[:, cols]
            sc_ref[rows, cols] = (u[rows, cols] * s2).astype(sc_ref.dtype)

    @pl.when(i == last)
    def _():
        z_ref[0] = z[tm - CHUNK:tm]

    glu = u_ref[:, 2 * D_SGU:2 * D_SGU + D_CONV] * jax.nn.sigmoid(u_ref[:, 2 * D_SGU + D_CONV:D_REST])

    @pl.when(i == 0)
    def _():
        hist[0:HALO, :] = jnp.zeros((HALO, D_CONV), F32)

    hist[HALO:HALO + tm, :] = glu
    y = jnp.zeros((tm, D_CONV), F32) + bdw_ref[...]
    for w in range(CONV_W):
        y = y + hist[pl.ds(HALO - (CONV_W - 1) + w, tm), :] * wdw_ref[pl.ds(w, 1), :]
    sc_ref[:, D_SGU:D_SGU + D_CONV] = jax.nn.silu(_ln(y, cg_ref[...], cb_ref[...])).astype(sc_ref.dtype)

    @pl.when(i == last)
    def _():
        cbuf_ref[0] = hist[pl.ds(HALO + tm - (CONV_W - 1), CONV_W - 1), :]

    hist[0:HALO, :] = hist[pl.ds(tm, HALO), :]


def _mix_prompt(u_all, lw, n_batch, seq, tm):
    nt = seq // tm
    kern = functools.partial(_mixp_kernel, tm=tm)
    full = lambda shape: pl.BlockSpec(shape, lambda b, i: (0,) * len(shape))
    return pl.pallas_call(
        kern,
        grid=(n_batch, nt),
        in_specs=[pl.BlockSpec((tm, D_REST), lambda b, i: (b * nt + i, 0)),
                  full((H_SGU, CHUNK, CHUNK)), full((CHUNK, D_SGU)), full((1, D_SGU)), full((1, D_SGU)),
                  full((CONV_W, D_CONV)), full((1, D_CONV)), full((1, D_CONV)), full((1, D_CONV))],
        out_specs=[pl.BlockSpec((tm, D_SGU + D_CONV), lambda b, i: (b * nt + i, 0)),
                   pl.BlockSpec((1, CHUNK, D_SGU), lambda b, i: (b, 0, 0)),
                   pl.BlockSpec((1, CONV_W - 1, D_CONV), lambda b, i: (b, 0, 0))],
        out_shape=[jax.ShapeDtypeStruct((n_batch * seq, D_SGU + D_CONV), BF16),
                   jax.ShapeDtypeStruct((n_batch, CHUNK, D_SGU), F32),
                   jax.ShapeDtypeStruct((n_batch, CONV_W - 1, D_CONV), F32)],
        scratch_shapes=[pltpu.VMEM((HALO + tm, D_CONV), F32)],
        compiler_params=_cparams(("arbitrary", "arbitrary")),
        name="mix_prompt",
    )(u_all, lw["w_s"], lw["bs_full"], lw["sgu_g"], lw["sgu_b"], lw["w_dw"], lw["b_dw"],
      lw["cln_g"], lw["cln_b"])


def _mixs_kernel(u_ref, st_ref, wv_ref, bs_ref, sg_ref, sb_ref, wsh_ref, wdw_ref, bdw_ref, cg_ref, cb_ref,
                 sc_ref, z_ref, glu_ref, *, n_dec, n_new):
    u = jax.nn.gelu(u_ref[:, 0:D_SGU])
    z = _ln(jax.nn.gelu(u_ref[:, D_SGU:2 * D_SGU]), sg_ref[...], sb_ref[...])
    z_ref[...] = z
    glu = u_ref[:, 2 * D_SGU:2 * D_SGU + D_CONV] * jax.nn.sigmoid(u_ref[:, 2 * D_SGU + D_CONV:D_REST])
    glu_ref[...] = glu
    st = st_ref[...]
    for i in range(n_new):
        rows = slice(i * n_dec, (i + 1) * n_dec)
        s = jnp.zeros((n_dec, D_SGU), F32) + bs_ref[pl.ds(i, 1), :]
        y = jnp.sum(st * wsh_ref[i][None], axis=1) + bdw_ref[...]
        for j in range(i + 1):
            src = slice(j * n_dec, (j + 1) * n_dec)
            s = s + wv_ref[pl.ds(i * n_new + j, 1), :] * z[src]
            y = y + glu[src] * wdw_ref[pl.ds(CONV_W - 1 - i + j, 1), :]
        sc_ref[rows, 0:D_SGU] = (u[rows] * s).astype(sc_ref.dtype)
        sc_ref[rows, D_SGU:D_SGU + D_CONV] = jax.nn.silu(_ln(y, cg_ref[...], cb_ref[...])).astype(sc_ref.dtype)


def _mix_sample(u_s, state, lw, n_dec, n_new):
    r = n_dec * n_new
    kern = functools.partial(_mixs_kernel, n_dec=n_dec, n_new=n_new)
    return pl.pallas_call(
        kern,
        out_shape=[jax.ShapeDtypeStruct((r, D_SGU + D_CONV), BF16),
                   jax.ShapeDtypeStruct((r, D_SGU), F32),
                   jax.ShapeDtypeStruct((r, D_CONV), F32)],
        compiler_params=pltpu.CompilerParams(vmem_limit_bytes=VMEM_LIMIT),
        name="mix_sample",
    )(u_s, state, lw["wv_s"], lw["bs_s"], lw["sgu_g"], lw["sgu_b"], lw["wsh"], lw["w_dw"], lw["b_dw"],
      lw["cln_g"], lw["cln_b"])


def _outproj_kernel(att_ref, sc_ref, x_ref, wo_ref, g_ref, b_ref, wrh_ref, wrl_ref, br_ref,
                    x1_ref, eidx_ref, rank_ref, gw_ref, cnt_ref, carry, *, tm, alpha):
    i = pl.program_id(0)
    mix = (jnp.dot(att_ref[...], wo_ref[0:D_ATT, :], preferred_element_type=F32)
           + jnp.dot(sc_ref[...], wo_ref[D_ATT:D_MODEL, :], preferred_element_type=F32))
    x1 = _ln(alpha * x_ref[...] + mix, g_ref[...], b_ref[...])
    x1_ref[...] = x1
    hi = x1.astype(BF16)
    lo = (x1 - hi.astype(F32)).astype(BF16)
    logits = (lax.dot_general(wrh_ref[...], hi, _NT, preferred_element_type=F32)
              + lax.dot_general(wrl_ref[...], hi, _NT, preferred_element_type=F32)
              + lax.dot_general(wrh_ref[...], lo, _NT, preferred_element_type=F32))
    scores = jax.nn.sigmoid(logits)
    biased = scores + br_ref[...]
    eio = lax.broadcasted_iota(I32, (GROUP_SIZE, tm), 0)
    big = 1 << 20
    sg = [scores[g * GROUP_SIZE:(g + 1) * GROUP_SIZE] for g in range(N_GROUPS)]
    bg = [biased[g * GROUP_SIZE:(g + 1) * GROUP_SIZE] for g in range(N_GROUPS)]
    gio = lax.broadcasted_iota(I32, (N_GROUPS, tm), 0)
    gscore = jnp.zeros((N_GROUPS, tm), F32)
    for g in range(N_GROUPS):
        m1 = jnp.max(bg[g], axis=0, keepdims=True)
        i1 = jnp.min(jnp.where(bg[g] == m1, eio, big), axis=0, keepdims=True)
        m2 = jnp.max(jnp.where(eio == i1, -jnp.inf, bg[g]), axis=0, keepdims=True)
        gscore = jnp.where(gio == g, m1 + m2, gscore)
    gsel = _select_topk_sublane(gscore, gio >= 0, gio, TOPK_GROUPS)
    mg = [jnp.where(gsel[g:g + 1] > 0.0, bg[g], -jnp.inf) for g in range(N_GROUPS)]
    sio = lax.broadcasted_iota(I32, (TOP_K, tm), 0)
    eidx = jnp.zeros((TOP_K, tm), I32)
    gws = jnp.zeros((TOP_K, tm), F32)
    selm = [jnp.zeros((GROUP_SIZE, tm), F32) for _ in range(N_GROUPS)]
    for s in range(TOP_K):
        m = jnp.max(mg[0], axis=0, keepdims=True)
        for g in range(1, N_GROUPS):
            m = jnp.maximum(m, jnp.max(mg[g], axis=0, keepdims=True))
        idx = jnp.full((1, tm), big, I32)
        for g in range(N_GROUPS):
            idx = jnp.minimum(idx, jnp.min(jnp.where(mg[g] == m, eio + g * GROUP_SIZE, big),
                                           axis=0, keepdims=True))
        sc = jnp.zeros((1, tm), F32)
        for g in range(N_GROUPS):
            pick = (eio + g * GROUP_SIZE) == idx
            sc = sc + jnp.sum(jnp.where(pick, sg[g], 0.0), axis=0, keepdims=True)
            mg[g] = jnp.where(pick, -jnp.inf, mg[g])
            selm[g] = jnp.where(pick, 1.0, selm[g])
        eidx = jnp.where(sio == s, idx, eidx)
        gws = jnp.where(sio == s, sc, gws)
    gw_ref[...] = gws / jnp.sum(gws, axis=0, keepdims=True) * ROUTE_SCALE
    eidx_ref[...] = eidx

    @pl.when(i == 0)
    def _():
        carry[...] = jnp.zeros(carry.shape, F32)

    sel_all = jnp.concatenate(selm, axis=0)
    t_r = lax.broadcasted_iota(I32, (tm, tm), 0)
    t_c = lax.broadcasted_iota(I32, (tm, tm), 1)
    before = jnp.where(t_r < t_c, 1.0, 0.0).astype(BF16)
    prior = jnp.dot(sel_all.astype(BF16), before, preferred_element_type=F32) + carry[:, 0:1]
    rank = jnp.zeros((TOP_K, tm), F32)
    for s in range(TOP_K):
        r_s = jnp.zeros((1, tm), F32)
        for g in range(N_GROUPS):
            pick = (eio + g * GROUP_SIZE) == eidx[s:s + 1]
            r_s = r_s + jnp.sum(jnp.where(pick, prior[g * GROUP_SIZE:(g + 1) * GROUP_SIZE], 0.0),
                                axis=0, keepdims=True)
        rank = jnp.where(sio == s, r_s, rank)
    rank_ref[...] = rank.astype(I32)
    carry[...] = carry[...] + jnp.sum(sel_all, axis=1, keepdims=True)
    cnt_ref[...] = carry[...]


def _outproj(att, sc, x, lw, tm, alpha):
    r = x.shape[0]
    kern = functools.partial(_outproj_kernel, tm=tm, alpha=alpha)
    row_blk = lambda w: pl.BlockSpec((tm, w), lambda i: (i, 0))
    col_blk = pl.BlockSpec((TOP_K, tm), lambda i: (0, i))
    full = lambda shape: pl.BlockSpec(shape, lambda i: (0,) * len(shape))
    return pl.pallas_call(
        kern,
        grid=(r // tm,),
        in_specs=[row_blk(D_ATT), row_blk(D_SGU + D_CONV), row_blk(D_MODEL), full((D_MODEL, D_MODEL)),
                  full((1, D_MODEL)), full((1, D_MODEL)), full((N_EXPERTS, D_MODEL)),
                  full((N_EXPERTS, D_MODEL)), full((N_EXPERTS, 1))],
        out_specs=[row_blk(D_MODEL), col_blk, col_blk, col_blk, full((N_EXPERTS, LANES))],
        out_shape=[jax.ShapeDtypeStruct((r, D_MODEL), F32),
                   jax.ShapeDtypeStruct((TOP_K, r), I32),
                   jax.ShapeDtypeStruct((TOP_K, r), I32),
                   jax.ShapeDtypeStruct((TOP_K, r), F32),
                   jax.ShapeDtypeStruct((N_EXPERTS, LANES), F32)],
        scratch_shapes=[pltpu.VMEM((N_EXPERTS, LANES), F32)],
        compiler_params=_cparams(("arbitrary",)),
        name="outproj_router",
    )(att, sc, x, lw["w_out"], lw["ln1_g"], lw["ln1_b"], lw["wr_hi"], lw["wr_lo"], lw["b_router"])


def _dispatch_kernel(dest_hbm, x_ref, xs_in, xs_out, idx, sem_idx, sem):
    del xs_in
    i = pl.program_id(0)
    cp = pltpu.make_async_copy(dest_hbm.at[i], idx, sem_idx)
    cp.start()
    cp.wait()

    def start(t, c):
        for s in range(TOP_K):
            pltpu.make_async_copy(x_ref.at[pl.ds(t, 1)], xs_out.at[pl.ds(idx[t * TOP_K + s], 1)],
                                  sem).start(priority=s % 2)
        return c

    lax.fori_loop(0, TOK_TILE, start, 0)
    for _ in range(TOP_K):
        pltpu.make_async_copy(x_ref, xs_out.at[pl.ds(0, TOK_TILE)], sem).wait()


def _dispatch(dest_tiles, x1, xs_buf):
    r = x1.shape[0]
    return pl.pallas_call(
        _dispatch_kernel,
        grid=(r // TOK_TILE,),
        in_specs=[pl.BlockSpec(memory_space=pl.ANY),
                  pl.BlockSpec((TOK_TILE, D_MODEL), lambda i: (i, 0)),
                  pl.BlockSpec(memory_space=pl.ANY)],
        out_specs=pl.BlockSpec(memory_space=pl.ANY),
        out_shape=jax.ShapeDtypeStruct(xs_buf.shape, xs_buf.dtype),
        scratch_shapes=[pltpu.SMEM((TOK_TILE * TOP_K,), I32),
                        pltpu.SemaphoreType.DMA(()), pltpu.SemaphoreType.DMA(())],
        input_output_aliases={2: 0},
        compiler_params=_cparams(("arbitrary",)),
        name="dispatch",
    )(dest_tiles, x1, xs_buf)


def _experts_kernel(be_ref, nu_ref, xs_ref, wgu_ref, wdn_ref, ys_ref, wgu_b, wdn_b):
    i = pl.program_id(0)
    prev = be_ref[jnp.maximum(i - 1, 0)]
    fresh = (i == 0) | (be_ref[i] != prev)

    @pl.when(fresh)
    def _():
        wgu_b[...] = wgu_ref[0, 0].astype(BF16)
        wdn_b[...] = wdn_ref[0, 0].astype(BF16)

    @pl.when(i < nu_ref[0])
    def _():
        h = jnp.dot(xs_ref[...].astype(BF16), wgu_b[...], preferred_element_type=F32)
        act = jax.nn.silu(h[:, 0:D_EXPERT]) * h[:, D_EXPERT:2 * D_EXPERT]
        ys_ref[...] = jnp.dot(act.astype(BF16), wdn_b[...], preferred_element_type=F32)

    @pl.when(i >= nu_ref[0])
    def _():
        ys_ref[...] = jnp.zeros(ys_ref.shape, F32)


def _experts(blk_e, n_used, xs, w_gu, w_dn, layer):
    rows = xs.shape[0]
    grid_spec = pltpu.PrefetchScalarGridSpec(
        num_scalar_prefetch=2,
        grid=(rows // EXPERT_ROWS,),
        in_specs=[pl.BlockSpec((EXPERT_ROWS, D_MODEL), lambda i, be, nu: (i, 0)),
                  pl.BlockSpec((1, 1, D_MODEL, 2 * D_EXPERT), lambda i, be, nu: (layer, be[i], 0, 0)),
                  pl.BlockSpec((1, 1, D_EXPERT, D_MODEL), lambda i, be, nu: (layer, be[i], 0, 0))],
        out_specs=pl.BlockSpec((EXPERT_ROWS, D_MODEL), lambda i, be, nu: (i, 0)),
        scratch_shapes=[pltpu.VMEM((D_MODEL, 2 * D_EXPERT), BF16), pltpu.VMEM((D_EXPERT, D_MODEL), BF16)])
    return pl.pallas_call(
        _experts_kernel,
        grid_spec=grid_spec,
        out_shape=jax.ShapeDtypeStruct((rows, D_MODEL), F32),
        compiler_params=_cparams(("arbitrary",)),
        name="experts",
    )(blk_e, n_used, xs, w_gu, w_dn)


def _combine_kernel(dest_hbm, gw_ref, x1_ref, ys_hbm, wsg_ref, wsd_ref, g_ref, b_ref, x2_ref,
                    idx, buf, sem_idx, sem, *, alpha):
    i = pl.program_id(0)
    cp = pltpu.make_async_copy(dest_hbm.at[i], idx, sem_idx)
    cp.start()
    cp.wait()

    def start(t, c):
        for s in range(TOP_K):
            pltpu.make_async_copy(ys_hbm.at[pl.ds(idx[t * TOP_K + s], 1)], buf.at[s, pl.ds(t, 1)],
                                  sem).start(priority=s % 2)
        return c

    lax.fori_loop(0, TOK_TILE, start, 0)
    x1 = x1_ref[...]
    h = jnp.dot(x1.astype(BF16), wsg_ref[...], preferred_element_type=F32)
    act = jax.nn.silu(h[:, 0:D_EXPERT]) * h[:, D_EXPERT:2 * D_EXPERT]
    f = jnp.dot(act.astype(BF16), wsd_ref[...], preferred_element_type=F32)
    for s in range(TOP_K):
        pltpu.make_async_copy(ys_hbm.at[pl.ds(0, TOK_TILE)], buf.at[s], sem).wait()
    gw = gw_ref[...]
    for s in range(TOP_K):
        f = f + gw[:, s:s + 1] * buf[s]
    x2_ref[...] = _ln(alpha * x1 + f, g_ref[...], b_ref[...])


def _combine(dest_tiles, gw_tok, x1, ys, lw, alpha):
    r = x1.shape[0]
    kern = functools.partial(_combine_kernel, alpha=alpha)
    full = lambda shape: pl.BlockSpec(shape, lambda i: (0,) * len(shape))
    return pl.pallas_call(
        kern,
        grid=(r // TOK_TILE,),
        in_specs=[pl.BlockSpec(memory_space=pl.ANY),
                  pl.BlockSpec((TOK_TILE, TOP_K), lambda i: (i, 0)),
                  pl.BlockSpec((TOK_TILE, D_MODEL), lambda i: (i, 0)),
                  pl.BlockSpec(memory_space=pl.ANY),
                  full((D_MODEL, 2 * D_EXPERT)), full((D_EXPERT, D_MODEL)),
                  full((1, D_MODEL)), full((1, D_MODEL))],
        out_specs=pl.BlockSpec((TOK_TILE, D_MODEL), lambda i: (i, 0)),
        out_shape=jax.ShapeDtypeStruct((r, D_MODEL), F32),
        scratch_shapes=[pltpu.SMEM((TOK_TILE * TOP_K,), I32),
                        pltpu.VMEM((TOP_K, TOK_TILE, D_MODEL), F32),
                        pltpu.SemaphoreType.DMA(()), pltpu.SemaphoreType.DMA(())],
        compiler_params=_cparams(("arbitrary",)),
        name="combine",
    )(dest_tiles, gw_tok, x1, ys, lw["ws_gu"], lw["ws_dn"], lw["ln2_g"], lw["ln2_b"])


def _bias_tables(rel_bias, n_new, past_len):
    far = rel_bias[_t5_bucket(jnp.asarray(MOBA_BLOCK + 1, I32))]
    b_io = jnp.arange(NUM_BUCKETS, dtype=I32)

    def lookup(dist):
        hit = (_t5_bucket(dist)[..., None, None] == b_io[:, None])
        return jnp.sum(jnp.where(hit, rel_bias, 0.0), axis=-2)

    qi = jnp.arange(Q_BLOCK, dtype=I32)[:, None]
    ko = jnp.arange(MOBA_BLOCK, dtype=I32)[None, :]
    k2 = jnp.arange(2 * MOBA_BLOCK, dtype=I32)[None, :]
    tiles = []
    for first in range(2):
        for qoff in range(MOBA_BLOCK // Q_BLOCK):
            dist = qoff * Q_BLOCK + qi - k2 + (1 - first) * MOBA_BLOCK
            t = lookup(dist) - far
            t = jnp.where((dist >= 0)[..., None], t, NEG)
            tiles.append(t.transpose(2, 0, 1))
    bt = jnp.stack(tiles, axis=1)
    bt = bt.reshape(N_PAIRS, 2, len(tiles), Q_BLOCK, 2 * MOBA_BLOCK)
    i_new = jnp.arange(NEW_PAD, dtype=I32)[:, None]
    last_start = (past_len // MOBA_BLOCK - 1) * MOBA_BLOCK
    dist_l = past_len + i_new - (last_start + ko)
    t_last = (lookup(dist_l) - far).transpose(2, 0, 1)
    kn = jnp.arange(LANES, dtype=I32)[None, :]
    dist_o = i_new - kn
    t_own = (lookup(dist_o) - far).transpose(2, 0, 1)
    t_own = jnp.where(((dist_o >= 0) & (kn < n_new))[None], t_own, NEG)
    return bt, t_last, t_own


def _layer_weights(l, w_in, w_out, w_s, b_s, sgu_ln_g, sgu_ln_b, w_dw, b_dw, conv_ln_g, conv_ln_b,
                   ln1_g, ln1_b, w_router, b_router, w_sh_gu, w_sh_dn, ln2_g, ln2_b, n_new):
    row = lambda a: a[l][None, :]
    ws = w_s[l]
    tril = jnp.tril(ws[:, :n_new, :n_new])
    wv_s = jnp.repeat(tril.transpose(1, 2, 0), HEAD_DIM, axis=-1).reshape(n_new * n_new, D_SGU)
    bs_s = jnp.repeat(b_s[l][:, :n_new].T, HEAD_DIM, axis=-1)
    bs_full = jnp.repeat(b_s[l].T, HEAD_DIM, axis=-1)
    r_io = jnp.arange(CONV_W - 1)[None, :]
    i_io = jnp.arange(n_new)[:, None]
    tap = r_io - i_io
    wsh = jnp.where((tap >= 0)[..., None], w_dw[l][jnp.clip(tap, 0, CONV_W - 1)], 0.0)
    wr = w_router[l].T
    wr_hi = wr.astype(BF16)
    return dict(
        w_in=w_in[l].astype(BF16), w_out=w_out[l].astype(BF16), w_s=ws, bs_full=bs_full,
        sgu_g=row(sgu_ln_g), sgu_b=row(sgu_ln_b), w_dw=w_dw[l], b_dw=row(b_dw),
        cln_g=row(conv_ln_g), cln_b=row(conv_ln_b), wv_s=wv_s, bs_s=bs_s, wsh=wsh,
        ln1_g=row(ln1_g), ln1_b=row(ln1_b), wr_hi=wr_hi, wr_lo=(wr - wr_hi.astype(F32)).astype(BF16),
        b_router=b_router[l][:, None], ws_gu=w_sh_gu[l].astype(BF16), ws_dn=w_sh_dn[l].astype(BF16),
        ln2_g=row(ln2_g), ln2_b=row(ln2_b))


@jax.jit
def _forward(x_prompt, x_sample, cache_k, cache_v, page_table, state_conv, rel_bias, w_in, w_out,
             w_s, b_s, sgu_ln_g, sgu_ln_b, w_dw, b_dw, conv_ln_g, conv_ln_b, ln1_g, ln1_b,
             w_router, b_router, w_exp_gu, w_exp_dn, w_sh_gu, w_sh_dn, ln2_g, ln2_b):
    depth = w_in.shape[0]
    n_batch, seq, _ = x_prompt.shape
    n_dec, n_new, _ = x_sample.shape
    n_pool = cache_k.shape[1]
    past_len = page_table.shape[1] * PAGE_SIZE
    alpha = (2 * depth) ** 0.25
    rp = n_batch * seq
    rs = n_dec * n_new
    r = rp + rs
    tm = 512 if (seq % 512 == 0 and rs % 512 == 0) else MOBA_BLOCK
    assert seq % tm == 0 and rs % tm == 0 and seq % MOBA_BLOCK == 0 and seq // MOBA_BLOCK <= HEAD_DIM
    assert (seq // MOBA_BLOCK) % 8 == 0 and past_len % MOBA_BLOCK == 0 and n_dec % 8 == 0
    assert MAX_DISTANCE <= MOBA_BLOCK and r % TOK_TILE == 0
    assert (seq // MOBA_BLOCK) % (2 * FAR_GROUP) == 0 and seq >= 2 * MOBA_BLOCK
    del n_pool

    n_assign = r * TOP_K
    n_blk = (n_assign + N_EXPERTS * (EXPERT_ROWS - 1) + EXPERT_ROWS - 1) // EXPERT_ROWS
    rows_pad = n_blk * EXPERT_ROWS

    bt, t_last, t_own = _bias_tables(rel_bias, n_new, past_len)
    cache_kt = cache_k.transpose(0, 1, 3, 4, 2)
    cache_vt = cache_v.transpose(0, 1, 3, 4, 2)

    def to_heads(a):
        a = a.reshape(n_new, n_dec, H_ATT, HEAD_DIM).transpose(1, 2, 0, 3)
        return jnp.pad(a, ((0, 0), (0, 0), (0, NEW_PAD - n_new), (0, 0)))

    x = jnp.concatenate([x_prompt.reshape(rp, D_MODEL),
                         x_sample.transpose(1, 0, 2).reshape(rs, D_MODEL)], axis=0)
    xs_buf = jnp.zeros((rows_pad, D_MODEL), F32)

    def to_bi(a):
        return a.reshape(n_new, n_dec, a.shape[-1]).transpose(1, 0, 2)

    outs = [[] for _ in range(8)]
    for l in range(depth):
        lw = _layer_weights(l, w_in, w_out, w_s, b_s, sgu_ln_g, sgu_ln_b, w_dw, b_dw, conv_ln_g,
                            conv_ln_b, ln1_g, ln1_b, w_router, b_router, w_sh_gu, w_sh_dn,
                            ln2_g, ln2_b, n_new)
        qb, kf, vf, ka, kb, vb, u_all, kmean = _inproj(x, lw["w_in"], tm, seq)
        kmean_p = kmean[:rp // MOBA_BLOCK].reshape(n_batch, seq // MOBA_BLOCK, D_ATT)
        att_p = _att_prompt(qb, ka, kb, vb, kmean_p, bt, n_batch, seq)
        k_s = to_bi(kf[rp:])
        v_s = to_bi(vf[rp:])
        att_s = _att_sample(page_table, to_heads(qb[rp:].astype(F32)), to_heads(kf[rp:]), to_heads(vf[rp:]),
                            cache_kt, cache_vt, l, t_last, t_own, n_new)
        att_s = att_s[:, :, :n_new].transpose(2, 0, 1, 3).reshape(rs, D_ATT).astype(BF16)
        sc_p, z_last, cbuf_p = _mix_prompt(u_all, lw, n_batch, seq, tm)
        sc_s, z_s, glu_s = _mix_sample(u_all[rp:], state_conv[l], lw, n_dec, n_new)
        att = jnp.concatenate([att_p, att_s], axis=0)
        sc = jnp.concatenate([sc_p, sc_s], axis=0)
        x1, eidx, rank, gw, cnt = _outproj(att, sc, x, lw, tm, alpha)
        counts = cnt[:, 0].astype(I32)
        padded = (counts + EXPERT_ROWS - 1) // EXPERT_ROWS * EXPERT_ROWS
        pend = jnp.cumsum(padded)
        pstart = pend - padded
        e_io = jnp.arange(N_EXPERTS, dtype=I32)
        pstart_of = jnp.sum(jnp.where(eidx[..., None] == e_io, pstart, 0), axis=-1)
        dest = (pstart_of + rank).T.reshape(r // TOK_TILE, TOK_TILE * TOP_K)
        blk_rows = jnp.arange(n_blk, dtype=I32)[:, None] * EXPERT_ROWS
        blk_e = jnp.minimum(jnp.sum((pend[None, :] <= blk_rows).astype(I32), axis=1), N_EXPERTS - 1)
        n_used = (pend[-1:] // EXPERT_ROWS).astype(I32)
        xs_buf = _dispatch(dest, x1, xs_buf)
        ys = _experts(blk_e, n_used, xs_buf, w_exp_gu, w_exp_dn, l)
        x = _combine(dest, gw.T, x1, ys, lw, alpha)

        outs[0].append(kf[:rp].reshape(n_batch, seq, H_ATT, HEAD_DIM))
        outs[1].append(vf[:rp].reshape(n_batch, seq, H_ATT, HEAD_DIM))
        outs[2].append(k_s.reshape(n_dec, n_new, H_ATT, HEAD_DIM))
        outs[3].append(v_s.reshape(n_dec, n_new, H_ATT, HEAD_DIM))
        outs[4].append(z_last)
        outs[5].append(to_bi(z_s))
        outs[6].append(cbuf_p)
        outs[7].append(jnp.concatenate([state_conv[l], to_bi(glu_s)], axis=1)[:, -(CONV_W - 1):])
    y_prompt = x[:rp].reshape(n_batch, seq, D_MODEL)
    y_sample = to_bi(x[rp:])
    return (y_prompt, y_sample) + tuple(jnp.stack(o) for o in outs)


def kernel(x_prompt, x_sample, cache_k, cache_v, page_table, state_conv, rel_bias, w_in, w_out, w_s, b_s,
           sgu_ln_g, sgu_ln_b, w_dw, b_dw, conv_ln_g, conv_ln_b, ln1_g, ln1_b, w_router, b_router,
           w_exp_gu, w_exp_dn, w_sh_gu, w_sh_dn, ln2_g, ln2_b):
    return _forward(x_prompt, x_sample, cache_k, cache_v, page_table, state_conv, rel_bias, w_in, w_out,
                    w_s, b_s, sgu_ln_g, sgu_ln_b, w_dw, b_dw, conv_ln_g, conv_ln_b, ln1_g, ln1_b,
                    w_router, b_router, w_exp_gu, w_exp_dn, w_sh_gu, w_sh_dn, ln2_g, ln2_b)
```

```python
import functools
import math

import jax
import jax.numpy as jnp
from jax import lax
from jax.experimental import pallas as pl
from jax.experimental.pallas import tpu as pltpu

F32 = jnp.float32
BF16 = jnp.bfloat16
I32 = jnp.int32
U32 = jnp.uint32

D_MODEL = 1024
HEAD_DIM = 64
H_ATT = 8
H_SGU = 4
D_ATT = H_ATT * HEAD_DIM
D_SGU = H_SGU * HEAD_DIM
D_CONV = D_MODEL - D_ATT - D_SGU
D_IN = 3 * D_ATT + 2 * D_SGU + 2 * D_CONV
D_REST = 2 * D_SGU + 2 * D_CONV
CHUNK = 128
CONV_W = 31
MOBA_BLOCK = 256
MOBA_TOPK = 3
Q_BLOCK = 256
NUM_BUCKETS = 32
MAX_DISTANCE = 128
N_EXPERTS = 64
TOP_K = 8
N_GROUPS = 8
GROUP_SIZE = N_EXPERTS // N_GROUPS
TOPK_GROUPS = 4
D_EXPERT = 256
ROUTE_SCALE = 2.5
LN_EPS = 1e-5
PAGE_SIZE = 128

HALF = D_MODEL // 2
LANES = 128
PAIR = 2 * HEAD_DIM
N_PAIRS = H_ATT // 2
NEG = -1e30
FAR_GROUP = 4
NEW_PAD = 8
SAMPLE_BLOCKS_PER_STEP = 4
EXPERT_ROWS = 256
TOK_TILE = 128
HALO = 32
VMEM_LIMIT = 56 * 1024 * 1024

_NT = (((1,), (1,)), ((), ()))


def _cparams(sem):
    return pltpu.CompilerParams(dimension_semantics=sem, vmem_limit_bytes=VMEM_LIMIT)


def _ln(x, g, b):
    mu = jnp.mean(x, axis=-1, keepdims=True)
    xc = x - mu
    var = jnp.mean(xc * xc, axis=-1, keepdims=True)
    return xc * lax.rsqrt(var + LN_EPS) * g + b


def _t5_bucket(dist):
    n = jnp.maximum(dist, 0)
    max_exact = NUM_BUCKETS // 2
    nf = jnp.maximum(n, 1).astype(F32)
    large = max_exact + (jnp.log(nf / max_exact) / math.log(MAX_DISTANCE / max_exact)
                         * (NUM_BUCKETS - max_exact)).astype(I32)
    large = jnp.minimum(large, NUM_BUCKETS - 1)
    return jnp.where(n < max_exact, n, large)


def _inproj_kernel(x_ref, w_ref, q_ref, k_ref, v_ref, ka_ref, kb_ref, vb_ref, u_ref, km_ref,
                   *, tm, tiles_per_seq):
    i = pl.program_id(0)
    xb = x_ref[...].astype(BF16)

    def seg(lo, hi):
        return jnp.dot(xb, w_ref[:, lo:hi], preferred_element_type=F32)

    q = seg(0, D_ATT)
    q_ref[...] = (q * (HEAD_DIM ** -0.5)).astype(BF16)
    k = seg(D_ATT, 2 * D_ATT)
    k_ref[...] = k
    v = seg(2 * D_ATT, 3 * D_ATT)
    v_ref[...] = v
    vb_ref[...] = v.astype(BF16)
    u_ref[...] = seg(3 * D_ATT, D_IN)
    for c in range(tm // MOBA_BLOCK):
        km_ref[c] = jnp.mean(k[c * MOBA_BLOCK:(c + 1) * MOBA_BLOCK], axis=0, keepdims=True)
    row = lax.broadcasted_iota(I32, (tm, D_ATT), 0)
    col = lax.broadcasted_iota(I32, (tm, D_ATT), 1)
    blk = lax.rem(i, tiles_per_seq) * (tm // MOBA_BLOCK) + row // MOBA_BLOCK
    within = col % PAIR
    ka_ref[...] = jnp.where(within < HEAD_DIM, k, jnp.where(within - HEAD_DIM == blk, 1.0, 0.0)).astype(BF16)
    kb_ref[...] = jnp.where(within >= HEAD_DIM, k, jnp.where(within == blk, 1.0, 0.0)).astype(BF16)


def _inproj(x, w_in_b, tm, seq):
    r = x.shape[0]
    n = r // tm
    kern = functools.partial(_inproj_kernel, tm=tm, tiles_per_seq=max(seq // tm, 1))
    row_blk = lambda w: pl.BlockSpec((tm, w), lambda i: (i, 0))
    return pl.pallas_call(
        kern,
        grid=(n,),
        in_specs=[row_blk(D_MODEL), pl.BlockSpec((D_MODEL, D_IN), lambda i: (0, 0))],
        out_specs=[row_blk(D_ATT)] * 6 + [row_blk(D_REST),
                   pl.BlockSpec((tm // MOBA_BLOCK, 1, D_ATT), lambda i: (i, 0, 0))],
        out_shape=[jax.ShapeDtypeStruct((r, D_ATT), BF16),
                   jax.ShapeDtypeStruct((r, D_ATT), F32),
                   jax.ShapeDtypeStruct((r, D_ATT), F32),
                   jax.ShapeDtypeStruct((r, D_ATT), BF16),
                   jax.ShapeDtypeStruct((r, D_ATT), BF16),
                   jax.ShapeDtypeStruct((r, D_ATT), BF16),
                   jax.ShapeDtypeStruct((r, D_REST), F32),
                   jax.ShapeDtypeStruct((r // MOBA_BLOCK, 1, D_ATT), F32)],
        compiler_params=_cparams(("arbitrary",)),
        name="inproj",
    )(x, w_in_b)


def _select_topk_sublane(g, valid, blk, k_sel):
    sel = jnp.zeros(g.shape, F32)
    g = jnp.where(valid, g, -jnp.inf)
    for _ in range(k_sel):
        m = jnp.max(g, axis=0, keepdims=True)
        cand = jnp.where((g == m) & valid & (sel == 0.0), blk, 1 << 20)
        idx = jnp.min(cand, axis=0, keepdims=True)
        pick = cand == idx
        pick = pick & (idx < (1 << 20))
        sel = jnp.where(pick, 1.0, sel)
        g = jnp.where(pick, -jnp.inf, g)
    return sel


def _attp_kernel(q_ref, ka_ref, kb_ref, v_ref, km_ref, bt_ref, o_ref, s_a, s_b, *, nb):
    qi = pl.program_id(2)
    q_per_blk = MOBA_BLOCK // Q_BLOCK
    own = qi // q_per_blk
    qoff = lax.rem(qi, q_per_blk)
    q2 = q_ref[...]
    lane = lax.broadcasted_iota(I32, (Q_BLOCK, PAIR), 1)
    km = km_ref[0]
    lane_k = lax.broadcasted_iota(I32, (nb, PAIR), 1)
    blk = lax.broadcasted_iota(I32, (nb, Q_BLOCK), 0)
    sels = []
    for h in range(2):
        in_head = (lane_k < HEAD_DIM) if h == 0 else (lane_k >= HEAD_DIM)
        kmh = jnp.where(in_head, km, 0.0)
        hi = kmh.astype(BF16)
        lo = (kmh - hi.astype(F32)).astype(BF16)
        g = (lax.dot_general(hi, q2, _NT, preferred_element_type=F32)
             + lax.dot_general(lo, q2, _NT, preferred_element_type=F32))
        sels.append(_select_topk_sublane(g, blk < own, blk, min(MOBA_TOPK, nb)))
    zpad = jnp.zeros((HEAD_DIM - nb, Q_BLOCK), F32)
    q2f = q2.astype(F32)

    def augment(keep):
        pens = [jnp.where(keep(sels[h]), 0.0, NEG) for h in range(2)]
        pen = jnp.concatenate([pens[1], zpad, pens[0], zpad], axis=0).T
        return (jnp.where(lane < HEAD_DIM, q2f, pen).astype(BF16),
                jnp.where(lane >= HEAD_DIM, q2f, pen).astype(BF16))

    qa_near = augment(lambda sel: (sel > 0.0) | (blk == own))
    qa_far = augment(lambda sel: (sel > 0.0) & (blk < own - 1))
    krefs = (ka_ref, kb_ref)

    def rows(j, n):
        return pl.ds(pl.multiple_of(j * MOBA_BLOCK, MOBA_BLOCK), n * MOBA_BLOCK)

    first = jnp.where(own == 0, 1, 0)
    near = rows(jnp.maximum(own - 1, 0), 2)
    vt = v_ref[near, :]
    state = []
    for h in range(2):
        s = lax.dot_general(qa_near[h], krefs[h][near, :], _NT, preferred_element_type=F32)
        s = s + bt_ref[0, h, pl.ds(first * q_per_blk + qoff, 1)][0]
        m = jnp.max(s, axis=1, keepdims=True)
        p = jnp.exp(s - m)
        state.append((m, jnp.sum(p, axis=1, keepdims=True),
                      jnp.dot(p.astype(BF16), vt, preferred_element_type=F32)))

    n_groups = nb // FAR_GROUP

    def far_scores(g, buf):
        slab = rows(g * FAR_GROUP, FAR_GROUP)
        for h in range(2):
            buf[h] = lax.dot_general(qa_far[h], krefs[h][slab, :], _NT, preferred_element_type=F32)

    def far_update(g, buf, st):
        vt = v_ref[rows(g * FAR_GROUP, FAR_GROUP), :]
        new = []
        for h in range(2):
            m, l, acc = st[h]
            s = buf[h]
            m_new = jnp.maximum(m, jnp.max(s, axis=1, keepdims=True))
            alpha = jnp.exp(m - m_new)
            p = jnp.exp(s - m_new)
            new.append((m_new, alpha * l + jnp.sum(p, axis=1, keepdims=True),
                        alpha * acc + jnp.dot(p.astype(BF16), vt, preferred_element_type=F32)))
        return tuple(new)

    def far_body(i, st):
        g = 2 * i
        far_scores(g + 1, s_b)
        st = far_update(g, s_a, st)
        far_scores(jnp.minimum(g + 2, n_groups - 1), s_a)
        return far_update(g + 1, s_b, st)

    n_far = jnp.maximum(own - 1, 0)
    far_scores(0, s_a)
    st = lax.fori_loop(0, (n_far + 2 * FAR_GROUP - 1) // (2 * FAR_GROUP), far_body, tuple(state))
    o0 = st[0][2] / st[0][1]
    o1 = st[1][2] / st[1][1]
    o_ref[...] = jnp.where(lane < HEAD_DIM, o0, o1).astype(o_ref.dtype)


def _att_prompt(qb, ka, kb, vb, kmean, bias_tiles, n_batch, seq):
    nb = seq // MOBA_BLOCK
    nq = seq // Q_BLOCK
    kern = functools.partial(_attp_kernel, nb=nb)
    seq_blk = pl.BlockSpec((seq, PAIR), lambda b, p, i: (b, p))
    return pl.pallas_call(
        kern,
        grid=(n_batch, N_PAIRS, nq),
        in_specs=[pl.BlockSpec((Q_BLOCK, PAIR), lambda b, p, i: (b * nq + i, p)),
                  seq_blk, seq_blk, seq_blk,
                  pl.BlockSpec((1, nb, PAIR), lambda b, p, i: (b, 0, p)),
                  pl.BlockSpec((1, 2, 2 * (MOBA_BLOCK // Q_BLOCK), Q_BLOCK, 2 * MOBA_BLOCK),
                               lambda b, p, i: (p, 0, 0, 0, 0))],
        out_specs=pl.BlockSpec((Q_BLOCK, PAIR), lambda b, p, i: (b * nq + i, p)),
        out_shape=jax.ShapeDtypeStruct((n_batch * seq, D_ATT), BF16),
        scratch_shapes=[pltpu.VMEM((2, Q_BLOCK, FAR_GROUP * MOBA_BLOCK), F32)] * 2,
        compiler_params=_cparams(("arbitrary", "arbitrary", "arbitrary")),
        name="att_prompt",
    )(qb, ka, kb, vb, kmean, bias_tiles)


def _atts_kernel(pt_ref, q_ref, kn_ref, vn_ref, *refs, nbp, n_new, bps):
    del pt_ref
    pages = bps * (MOBA_BLOCK // PAGE_SIZE)
    k_refs, v_refs = refs[:pages], refs[pages:2 * pages]
    tl_ref, to_ref, o_ref, s_buf, v_buf, g_buf = refs[2 * pages:]
    j = pl.program_id(1)
    hqk = (((2,), (1,)), ((0,), (0,)))
    hpv = (((2,), (2,)), ((0,), (0,)))
    q = q_ref[0]
    qb = q.astype(BF16)
    lane = lax.broadcasted_iota(I32, (H_ATT, NEW_PAD, LANES), 2)

    @pl.when(j == 0)
    def _():
        g_buf[...] = jnp.full(g_buf.shape, -jnp.inf, F32)

    for c in range(bps):
        jb = j * bps + c
        kt = jnp.concatenate([k_refs[2 * c][0, 0], k_refs[2 * c + 1][0, 0]], axis=2).astype(BF16)
        s = lax.dot_general(qb, kt, hqk, preferred_element_type=F32)
        s_buf[jb] = s
        v_buf[jb] = jnp.concatenate([v_refs[2 * c][0, 0], v_refs[2 * c + 1][0, 0]], axis=2).astype(BF16)
        gj = jnp.sum(s, axis=2, keepdims=True)
        g_buf[...] = jnp.where(lane == jb, gj, g_buf[...])

    @pl.when(j == nbp // bps - 1)
    def _():
        g = g_buf[...]
        valid = lane < nbp
        sel = jnp.zeros(g.shape, F32)
        for _ in range(min(MOBA_TOPK, nbp + 1)):
            m = jnp.max(g, axis=2, keepdims=True)
            cand = jnp.where((g == m) & valid & (sel == 0.0), lane, 1 << 20)
            idx = jnp.min(cand, axis=2, keepdims=True)
            pick = (cand == idx) & (idx < (1 << 20))
            sel = jnp.where(pick, 1.0, sel)
            g = jnp.where(pick, -jnp.inf, g)
        pen = jnp.where(sel > 0.0, 0.0, NEG)
        kn = kn_ref[0]
        vn = vn_ref[0]
        t_own = to_ref[...]
        s_own = [jnp.sum(q * kn[:, i:i + 1, :], axis=2, keepdims=True) + t_own[:, :, i:i + 1]
                 for i in range(n_new)]
        m = s_own[0]
        for i in range(1, n_new):
            m = jnp.maximum(m, s_own[i])
        for jj in range(nbp):
            sj = s_buf[jj] + pen[:, :, jj:jj + 1]
            if jj == nbp - 1:
                sj = sj + tl_ref[...]
            s_buf[jj] = sj
            m = jnp.maximum(m, jnp.max(sj, axis=2, keepdims=True))
        l = jnp.zeros(m.shape, F32)
        acc = jnp.zeros(q.shape, F32)
        for i in range(n_new):
            p = jnp.exp(s_own[i] - m)
            l = l + p
            acc = acc + p * vn[:, i:i + 1, :]
        for jj in range(nbp):
            p = jnp.exp(s_buf[jj] - m)
            l = l + jnp.sum(p, axis=2, keepdims=True)
            acc = acc + lax.dot_general(p.astype(BF16), v_buf[jj], hpv, preferred_element_type=F32)
        o_ref[0] = acc / l


def _att_sample(page_table, q_s, k_new, v_new, cache_kt, cache_vt, layer, t_last, t_own, n_new):
    n_dec = q_s.shape[0]
    n_pages = page_table.shape[1]
    pages_per_blk = MOBA_BLOCK // PAGE_SIZE
    nbp = n_pages // pages_per_blk
    bps = math.gcd(nbp, SAMPLE_BLOCKS_PER_STEP)
    pages = bps * pages_per_blk
    kern = functools.partial(_atts_kernel, nbp=nbp, n_new=n_new, bps=bps)
    new_blk = pl.BlockSpec((1, H_ATT, NEW_PAD, HEAD_DIM), lambda b, j, pt: (b, 0, 0, 0))

    def page_spec(which):
        return pl.BlockSpec((1, 1, H_ATT, HEAD_DIM, PAGE_SIZE),
                            lambda b, j, pt: (layer, pt[b, pages * j + which], 0, 0, 0))

    page_specs = [page_spec(w) for w in range(pages)]
    grid_spec = pltpu.PrefetchScalarGridSpec(
        num_scalar_prefetch=1,
        grid=(n_dec, nbp // bps),
        in_specs=[new_blk, new_blk, new_blk] + page_specs + page_specs + [
                  pl.BlockSpec((H_ATT, NEW_PAD, MOBA_BLOCK), lambda b, j, pt: (0, 0, 0)),
                  pl.BlockSpec((H_ATT, NEW_PAD, LANES), lambda b, j, pt: (0, 0, 0))],
        out_specs=new_blk,
        scratch_shapes=[pltpu.VMEM((nbp, H_ATT, NEW_PAD, MOBA_BLOCK), F32),
                        pltpu.VMEM((nbp, H_ATT, HEAD_DIM, MOBA_BLOCK), BF16),
                        pltpu.VMEM((H_ATT, NEW_PAD, LANES), F32)])
    return pl.pallas_call(
        kern,
        grid_spec=grid_spec,
        out_shape=jax.ShapeDtypeStruct((n_dec, H_ATT, NEW_PAD, HEAD_DIM), F32),
        compiler_params=_cparams(("arbitrary", "arbitrary")),
        name="att_sample",
    )(page_table, q_s, k_new, v_new, *([cache_kt] * pages), *([cache_vt] * pages), t_last, t_own)


def _mixp_kernel(u_ref, ws_ref, bs_ref, sg_ref, sb_ref, wdw_ref, bdw_ref, cg_ref, cb_ref,
                 sc_ref, z_ref, cbuf_ref, hist, *, tm):
    i = pl.program_id(1)
    last = pl.num_programs(1) - 1
    u = jax.nn.gelu(u_ref[:, 0:D_SGU])
    z = _ln(jax.nn.gelu(u_ref[:, D_SGU:2 * D_SGU]), sg_ref[...], sb_ref[...])
    tri_r = lax.broadcasted_iota(I32, (CHUNK, CHUNK), 0)
    tri_c = lax.broadcasted_iota(I32, (CHUNK, CHUNK), 1)
    lane = lax.broadcasted_iota(I32, (CHUNK, PAIR), 1)
    w_tril = [jnp.where(tri_c <= tri_r, ws_ref[h], 0.0).astype(BF16) for h in range(H_SGU)]
    for c in range(tm // CHUNK):
        rows = slice(c * CHUNK, (c + 1) * CHUNK)
        for pp in range(H_SGU // 2):
            cols = slice(pp * PAIR, (pp + 1) * PAIR)
            z2 = z[rows, cols].astype(BF16)
            s0 = jnp.dot(w_tril[2 * pp], z2, preferred_element_type=F32)
            s1 = jnp.dot(w_tril[2 * pp + 1], z2, preferred_element_type=F32)
            s2 = jnp.where(lane < HEAD_DIM, s0, s1) + bs_ref[:, cols]
            sc_ref[rows, cols] = (u[rows, cols] * s2).astype(sc_ref.dtype)

    @pl.when(i == last)
    def _():
        z_ref[0] = z[tm - CHUNK:tm]

    glu = u_ref[:, 2 * D_SGU:2 * D_SGU + D_CONV] * jax.nn.sigmoid(u_ref[:, 2 * D_SGU + D_CONV:D_REST])

    @pl.when(i == 0)
    def _():
        hist[0:HALO, :] = jnp.zeros((HALO, D_CONV), F32)

    hist[HALO:HALO + tm, :] = glu
    y = jnp.zeros((tm, D_CONV), F32) + bdw_ref[...]
    for w in range(CONV_W):
        y = y + hist[pl.ds(HALO - (CONV_W - 1) + w, tm), :] * wdw_ref[pl.ds(w, 1), :]
    sc_ref[:, D_SGU:D_SGU + D_CONV] = jax.nn.silu(_ln(y, cg_ref[...], cb_ref[...])).astype(sc_ref.dtype)

    @pl.when(i == last)
    def _():
        cbuf_ref[0] = hist[pl.ds(HALO + tm - (CONV_W - 1), CONV_W - 1), :]

    hist[0:HALO, :] = hist[pl.ds(tm, HALO), :]


def _mix_prompt(u_all, lw, n_batch, seq, tm):
    nt = seq // tm
    kern = functools.partial(_mixp_kernel, tm=tm)
    full = lambda shape: pl.BlockSpec(shape, lambda b, i: (0,) * len(shape))
    return pl.pallas_call(
        kern,
        grid=(n_batch, nt),
        in_specs=[pl.BlockSpec((tm, D_REST), lambda b, i: (b * nt + i, 0)),
                  full((H_SGU, CHUNK, CHUNK)), full((CHUNK, D_SGU)), full((1, D_SGU)), full((1, D_SGU)),
                  full((CONV_W, D_CONV)), full((1, D_CONV)), full((1, D_CONV)), full((1, D_CONV))],
        out_specs=[pl.BlockSpec((tm, D_SGU + D_CONV), lambda b, i: (b * nt + i, 0)),
                   pl.BlockSpec((1, CHUNK, D_SGU), lambda b, i: (b, 0, 0)),
                   pl.BlockSpec((1, CONV_W - 1, D_CONV), lambda b, i: (b, 0, 0))],
        out_shape=[jax.ShapeDtypeStruct((n_batch * seq, D_SGU + D_CONV), BF16),
                   jax.ShapeDtypeStruct((n_batch, CHUNK, D_SGU), F32),
                   jax.ShapeDtypeStruct((n_batch, CONV_W - 1, D_CONV), F32)],
        scratch_shapes=[pltpu.VMEM((HALO + tm, D_CONV), F32)],
        compiler_params=_cparams(("arbitrary", "arbitrary")),
        name="mix_prompt",
    )(u_all, lw["w_s"], lw["bs_full"], lw["sgu_g"], lw["sgu_b"], lw["w_dw"], lw["b_dw"],
      lw["cln_g"], lw["cln_b"])


def _mixs_kernel(u_ref, st_ref, wv_ref, bs_ref, sg_ref, sb_ref, wsh_ref, wdw_ref, bdw_ref, cg_ref, cb_ref,
                 sc_ref, z_ref, glu_ref, *, n_dec, n_new):
    u = jax.nn.gelu(u_ref[:, 0:D_SGU])
    z = _ln(jax.nn.gelu(u_ref[:, D_SGU:2 * D_SGU]), sg_ref[...], sb_ref[...])
    z_ref[...] = z
    glu = u_ref[:, 2 * D_SGU:2 * D_SGU + D_CONV] * jax.nn.sigmoid(u_ref[:, 2 * D_SGU + D_CONV:D_REST])
    glu_ref[...] = glu
    st = st_ref[...]
    for i in range(n_new):
        rows = slice(i * n_dec, (i + 1) * n_dec)
        s = jnp.zeros((n_dec, D_SGU), F32) + bs_ref[pl.ds(i, 1), :]
        y = jnp.sum(st * wsh_ref[i][None], axis=1) + bdw_ref[...]
        for j in range(i + 1):
            src = slice(j * n_dec, (j + 1) * n_dec)
            s = s + wv_ref[pl.ds(i * n_new + j, 1), :] * z[src]
            y = y + glu[src] * wdw_ref[pl.ds(CONV_W - 1 - i + j, 1), :]
        sc_ref[rows, 0:D_SGU] = (u[rows] * s).astype(sc_ref.dtype)
        sc_ref[rows, D_SGU:D_SGU + D_CONV] = jax.nn.silu(_ln(y, cg_ref[...], cb_ref[...])).astype(sc_ref.dtype)


def _mix_sample(u_s, state, lw, n_dec, n_new):
    r = n_dec * n_new
    kern = functools.partial(_mixs_kernel, n_dec=n_dec, n_new=n_new)
    return pl.pallas_call(
        kern,
        out_shape=[jax.ShapeDtypeStruct((r, D_SGU + D_CONV), BF16),
                   jax.ShapeDtypeStruct((r, D_SGU), F32),
                   jax.ShapeDtypeStruct((r, D_CONV), F32)],
        compiler_params=pltpu.CompilerParams(vmem_limit_bytes=VMEM_LIMIT),
        name="mix_sample",
    )(u_s, state, lw["wv_s"], lw["bs_s"], lw["sgu_g"], lw["sgu_b"], lw["wsh"], lw["w_dw"], lw["b_dw"],
      lw["cln_g"], lw["cln_b"])


def _pack_rows(x):
    xb = x.astype(BF16).astype(F32)
    hi = lax.bitcast_convert_type(xb[:, :HALF], U32)
    lo = lax.bitcast_convert_type(xb[:, HALF:], U32)
    return hi | lax.shift_right_logical(lo, jnp.full(lo.shape, 16, U32))


def _unpack_rows(w):
    a = lax.bitcast_convert_type(w & jnp.full(w.shape, 0xFFFF0000, U32), F32)
    b = lax.bitcast_convert_type(lax.shift_left(w, jnp.full(w.shape, 16, U32)), F32)
    return a, b


def _outproj_kernel(att_ref, sc_ref, x_ref, wo_ref, g_ref, b_ref, wrh_ref, wrl_ref, br_ref,
                    x1_ref, x1p_ref, eidx_ref, rank_ref, gw_ref, cnt_ref, carry, *, tm, alpha):
    i = pl.program_id(0)
    mix = (jnp.dot(att_ref[...], wo_ref[0:D_ATT, :], preferred_element_type=F32)
           + jnp.dot(sc_ref[...], wo_ref[D_ATT:D_MODEL, :], preferred_element_type=F32))
    x1 = _ln(alpha * x_ref[...] + mix, g_ref[...], b_ref[...])
    x1_ref[...] = x1
    x1p_ref[...] = _pack_rows(x1)
    hi = x1.astype(BF16)
    lo = (x1 - hi.astype(F32)).astype(BF16)
    logits = (lax.dot_general(wrh_ref[...], hi, _NT, preferred_element_type=F32)
              + lax.dot_general(wrl_ref[...], hi, _NT, preferred_element_type=F32)
              + lax.dot_general(wrh_ref[...], lo, _NT, preferred_element_type=F32))
    scores = jax.nn.sigmoid(logits)
    biased = scores + br_ref[...]
    eio = lax.broadcasted_iota(I32, (GROUP_SIZE, tm), 0)
    big = 1 << 20
    sg = [scores[g * GROUP_SIZE:(g + 1) * GROUP_SIZE] for g in range(N_GROUPS)]
    bg = [biased[g * GROUP_SIZE:(g + 1) * GROUP_SIZE] for g in range(N_GROUPS)]
    gio = lax.broadcasted_iota(I32, (N_GROUPS, tm), 0)
    gscore = jnp.zeros((N_GROUPS, tm), F32)
    for g in range(N_GROUPS):
        m1 = jnp.max(bg[g], axis=0, keepdims=True)
        i1 = jnp.min(jnp.where(bg[g] == m1, eio, big), axis=0, keepdims=True)
        m2 = jnp.max(jnp.where(eio == i1, -jnp.inf, bg[g]), axis=0, keepdims=True)
        gscore = jnp.where(gio == g, m1 + m2, gscore)
    gsel = _select_topk_sublane(gscore, gio >= 0, gio, TOPK_GROUPS)
    mg = [jnp.where(gsel[g:g + 1] > 0.0, bg[g], -jnp.inf) for g in range(N_GROUPS)]
    sio = lax.broadcasted_iota(I32, (TOP_K, tm), 0)
    eidx = jnp.zeros((TOP_K, tm), I32)
    gws = jnp.zeros((TOP_K, tm), F32)
    selm = [jnp.zeros((GROUP_SIZE, tm), F32) for _ in range(N_GROUPS)]
    for s in range(TOP_K):
        m = jnp.max(mg[0], axis=0, keepdims=True)
        for g in range(1, N_GROUPS):
            m = jnp.maximum(m, jnp.max(mg[g], axis=0, keepdims=True))
        idx = jnp.full((1, tm), big, I32)
        for g in range(N_GROUPS):
            idx = jnp.minimum(idx, jnp.min(jnp.where(mg[g] == m, eio + g * GROUP_SIZE, big),
                                           axis=0, keepdims=True))
        sc = jnp.zeros((1, tm), F32)
        for g in range(N_GROUPS):
            pick = (eio + g * GROUP_SIZE) == idx
            sc = sc + jnp.sum(jnp.where(pick, sg[g], 0.0), axis=0, keepdims=True)
            mg[g] = jnp.where(pick, -jnp.inf, mg[g])
            selm[g] = jnp.where(pick, 1.0, selm[g])
        eidx = jnp.where(sio == s, idx, eidx)
        gws = jnp.where(sio == s, sc, gws)
    gw_ref[...] = gws / jnp.sum(gws, axis=0, keepdims=True) * ROUTE_SCALE
    eidx_ref[...] = eidx

    @pl.when(i == 0)
    def _():
        carry[...] = jnp.zeros(carry.shape, F32)

    sel_all = jnp.concatenate(selm, axis=0)
    t_r = lax.broadcasted_iota(I32, (tm, tm), 0)
    t_c = lax.broadcasted_iota(I32, (tm, tm), 1)
    before = jnp.where(t_r < t_c, 1.0, 0.0).astype(BF16)
    prior = jnp.dot(sel_all.astype(BF16), before, preferred_element_type=F32) + carry[:, 0:1]
    rank = jnp.zeros((TOP_K, tm), F32)
    for s in range(TOP_K):
        r_s = jnp.zeros((1, tm), F32)
        for g in range(N_GROUPS):
            pick = (eio + g * GROUP_SIZE) == eidx[s:s + 1]
            r_s = r_s + jnp.sum(jnp.where(pick, prior[g * GROUP_SIZE:(g + 1) * GROUP_SIZE], 0.0),
                                axis=0, keepdims=True)
        rank = jnp.where(sio == s, r_s, rank)
    rank_ref[...] = rank.astype(I32)
    carry[...] = carry[...] + jnp.sum(sel_all, axis=1, keepdims=True)
    cnt_ref[...] = carry[...]


def _outproj(att, sc, x, lw, tm, alpha):
    r = x.shape[0]
    kern = functools.partial(_outproj_kernel, tm=tm, alpha=alpha)
    row_blk = lambda w: pl.BlockSpec((tm, w), lambda i: (i, 0))
    col_blk = pl.BlockSpec((TOP_K, tm), lambda i: (0, i))
    full = lambda shape: pl.BlockSpec(shape, lambda i: (0,) * len(shape))
    return pl.pallas_call(
        kern,
        grid=(r // tm,),
        in_specs=[row_blk(D_ATT), row_blk(D_SGU + D_CONV), row_blk(D_MODEL), full((D_MODEL, D_MODEL)),
                  full((1, D_MODEL)), full((1, D_MODEL)), full((N_EXPERTS, D_MODEL)),
                  full((N_EXPERTS, D_MODEL)), full((N_EXPERTS, 1))],
        out_specs=[row_blk(D_MODEL), row_blk(HALF), col_blk, col_blk, col_blk, full((N_EXPERTS, LANES))],
        out_shape=[jax.ShapeDtypeStruct((r, D_MODEL), F32),
                   jax.ShapeDtypeStruct((r, HALF), U32),
                   jax.ShapeDtypeStruct((TOP_K, r), I32),
                   jax.ShapeDtypeStruct((TOP_K, r), I32),
                   jax.ShapeDtypeStruct((TOP_K, r), F32),
                   jax.ShapeDtypeStruct((N_EXPERTS, LANES), F32)],
        scratch_shapes=[pltpu.VMEM((N_EXPERTS, LANES), F32)],
        compiler_params=_cparams(("arbitrary",)),
        name="outproj_router",
    )(att, sc, x, lw["w_out"], lw["ln1_g"], lw["ln1_b"], lw["wr_hi"], lw["wr_lo"], lw["b_router"])


def _dispatch_kernel(dest_hbm, x_ref, xs_in, xs_out, idx, sem_idx, sem):
    del xs_in
    i = pl.program_id(0)
    cp = pltpu.make_async_copy(dest_hbm.at[i], idx, sem_idx)
    cp.start()
    cp.wait()

    def start(t, c):
        for s in range(TOP_K):
            pltpu.make_async_copy(x_ref.at[pl.ds(t, 1)], xs_out.at[pl.ds(idx[t * TOP_K + s], 1)],
                                  sem).start(priority=s % 2)
        return c

    lax.fori_loop(0, TOK_TILE, start, 0)
    for _ in range(TOP_K):
        pltpu.make_async_copy(x_ref, xs_out.at[pl.ds(0, TOK_TILE)], sem).wait()


def _dispatch(dest_tiles, x1p, xs_buf):
    r = x1p.shape[0]
    return pl.pallas_call(
        _dispatch_kernel,
        grid=(r // TOK_TILE,),
        in_specs=[pl.BlockSpec(memory_space=pl.ANY),
                  pl.BlockSpec((TOK_TILE, HALF), lambda i: (i, 0)),
                  pl.BlockSpec(memory_space=pl.ANY)],
        out_specs=pl.BlockSpec(memory_space=pl.ANY),
        out_shape=jax.ShapeDtypeStruct(xs_buf.shape, xs_buf.dtype),
        scratch_shapes=[pltpu.SMEM((TOK_TILE * TOP_K,), I32),
                        pltpu.SemaphoreType.DMA(()), pltpu.SemaphoreType.DMA(())],
        input_output_aliases={2: 0},
        compiler_params=_cparams(("arbitrary",)),
        name="dispatch",
    )(dest_tiles, x1p, xs_buf)


def _experts_kernel(be_ref, nu_ref, xs_ref, wgu_ref, wdn_ref, ys_ref, wgu_b, wdn_b):
    i = pl.program_id(0)
    prev = be_ref[jnp.maximum(i - 1, 0)]
    fresh = (i == 0) | (be_ref[i] != prev)

    @pl.when(fresh)
    def _():
        wgu_b[...] = wgu_ref[0, 0].astype(BF16)
        wdn_b[...] = wdn_ref[0, 0].astype(BF16)

    @pl.when(i < nu_ref[0])
    def _():
        xa, xb = _unpack_rows(xs_ref[...])
        h = (jnp.dot(xa.astype(BF16), wgu_b[0:HALF, :], preferred_element_type=F32)
             + jnp.dot(xb.astype(BF16), wgu_b[HALF:D_MODEL, :], preferred_element_type=F32))
        act = jax.nn.silu(h[:, 0:D_EXPERT]) * h[:, D_EXPERT:2 * D_EXPERT]
        ys_ref[...] = _pack_rows(jnp.dot(act.astype(BF16), wdn_b[...], preferred_element_type=F32))

    @pl.when(i >= nu_ref[0])
    def _():
        ys_ref[...] = jnp.zeros(ys_ref.shape, U32)


def _experts(blk_e, n_used, xs, w_gu, w_dn, layer):
    rows = xs.shape[0]
    grid_spec = pltpu.PrefetchScalarGridSpec(
        num_scalar_prefetch=2,
        grid=(rows // EXPERT_ROWS,),
        in_specs=[pl.BlockSpec((EXPERT_ROWS, HALF), lambda i, be, nu: (i, 0)),
                  pl.BlockSpec((1, 1, D_MODEL, 2 * D_EXPERT), lambda i, be, nu: (layer, be[i], 0, 0)),
                  pl.BlockSpec((1, 1, D_EXPERT, D_MODEL), lambda i, be, nu: (layer, be[i], 0, 0))],
        out_specs=pl.BlockSpec((EXPERT_ROWS, HALF), lambda i, be, nu: (i, 0)),
        scratch_shapes=[pltpu.VMEM((D_MODEL, 2 * D_EXPERT), BF16), pltpu.VMEM((D_EXPERT, D_MODEL), BF16)])
    return pl.pallas_call(
        _experts_kernel,
        grid_spec=grid_spec,
        out_shape=jax.ShapeDtypeStruct((rows, HALF), U32),
        compiler_params=_cparams(("arbitrary",)),
        name="experts",
    )(blk_e, n_used, xs, w_gu, w_dn)


def _combine_kernel(dest_hbm, gw_ref, x1_ref, ys_hbm, wsg_ref, wsd_ref, g_ref, b_ref, x2_ref,
                    idx, buf, sem_idx, sem, *, alpha):
    i = pl.program_id(0)
    cp = pltpu.make_async_copy(dest_hbm.at[i], idx, sem_idx)
    cp.start()
    cp.wait()

    def start(t, c):
        for s in range(TOP_K):
            pltpu.make_async_copy(ys_hbm.at[pl.ds(idx[t * TOP_K + s], 1)], buf.at[s, pl.ds(t, 1)],
                                  sem).start(priority=s % 2)
        return c

    lax.fori_loop(0, TOK_TILE, start, 0)
    x1 = x1_ref[...]
    h = jnp.dot(x1.astype(BF16), wsg_ref[...], preferred_element_type=F32)
    act = jax.nn.silu(h[:, 0:D_EXPERT]) * h[:, D_EXPERT:2 * D_EXPERT]
    f = jnp.dot(act.astype(BF16), wsd_ref[...], preferred_element_type=F32)
    for s in range(TOP_K):
        pltpu.make_async_copy(ys_hbm.at[pl.ds(0, TOK_TILE)], buf.at[s], sem).wait()
    gw = gw_ref[...]
    fa = f[:, :HALF]
    fb = f[:, HALF:]
    for s in range(TOP_K):
        ya, yb = _unpack_rows(buf[s])
        fa = fa + gw[:, s:s + 1] * ya
        fb = fb + gw[:, s:s + 1] * yb
    x2_ref[...] = _ln(alpha * x1 + jnp.concatenate([fa, fb], axis=1), g_ref[...], b_ref[...])


def _combine(dest_tiles, gw_tok, x1, ys, lw, alpha):
    r = x1.shape[0]
    kern = functools.partial(_combine_kernel, alpha=alpha)
    full = lambda shape: pl.BlockSpec(shape, lambda i: (0,) * len(shape))
    return pl.pallas_call(
        kern,
        grid=(r // TOK_TILE,),
        in_specs=[pl.BlockSpec(memory_space=pl.ANY),
                  pl.BlockSpec((TOK_TILE, TOP_K), lambda i: (i, 0)),
                  pl.BlockSpec((TOK_TILE, D_MODEL), lambda i: (i, 0)),
                  pl.BlockSpec(memory_space=pl.ANY),
                  full((D_MODEL, 2 * D_EXPERT)), full((D_EXPERT, D_MODEL)),
                  full((1, D_MODEL)), full((1, D_MODEL))],
        out_specs=pl.BlockSpec((TOK_TILE, D_MODEL), lambda i: (i, 0)),
        out_shape=jax.ShapeDtypeStruct((r, D_MODEL), F32),
        scratch_shapes=[pltpu.SMEM((TOK_TILE * TOP_K,), I32),
                        pltpu.VMEM((TOP_K, TOK_TILE, HALF), U32),
                        pltpu.SemaphoreType.DMA(()), pltpu.SemaphoreType.DMA(())],
        compiler_params=_cparams(("arbitrary",)),
        name="combine",
    )(dest_tiles, gw_tok, x1, ys, lw["ws_gu"], lw["ws_dn"], lw["ln2_g"], lw["ln2_b"])


def _bias_tables(rel_bias, n_new, past_len):
    far = rel_bias[_t5_bucket(jnp.asarray(MOBA_BLOCK + 1, I32))]
    b_io = jnp.arange(NUM_BUCKETS, dtype=I32)

    def lookup(dist):
        hit = (_t5_bucket(dist)[..., None, None] == b_io[:, None])
        return jnp.sum(jnp.where(hit, rel_bias, 0.0), axis=-2)

    qi = jnp.arange(Q_BLOCK, dtype=I32)[:, None]
    ko = jnp.arange(MOBA_BLOCK, dtype=I32)[None, :]
    k2 = jnp.arange(2 * MOBA_BLOCK, dtype=I32)[None, :]
    tiles = []
    for first in range(2):
        for qoff in range(MOBA_BLOCK // Q_BLOCK):
            dist = qoff * Q_BLOCK + qi - k2 + (1 - first) * MOBA_BLOCK
            t = lookup(dist) - far
            t = jnp.where((dist >= 0)[..., None], t, NEG)
            tiles.append(t.transpose(2, 0, 1))
    bt = jnp.stack(tiles, axis=1)
    bt = bt.reshape(N_PAIRS, 2, len(tiles), Q_BLOCK, 2 * MOBA_BLOCK)
    i_new = jnp.arange(NEW_PAD, dtype=I32)[:, None]
    last_start = (past_len // MOBA_BLOCK - 1) * MOBA_BLOCK
    dist_l = past_len + i_new - (last_start + ko)
    t_last = (lookup(dist_l) - far).transpose(2, 0, 1)
    kn = jnp.arange(LANES, dtype=I32)[None, :]
    dist_o = i_new - kn
    t_own = (lookup(dist_o) - far).transpose(2, 0, 1)
    t_own = jnp.where(((dist_o >= 0) & (kn < n_new))[None], t_own, NEG)
    return bt, t_last, t_own


def _layer_weights(l, w_in, w_out, w_s, b_s, sgu_ln_g, sgu_ln_b, w_dw, b_dw, conv_ln_g, conv_ln_b,
                   ln1_g, ln1_b, w_router, b_router, w_sh_gu, w_sh_dn, ln2_g, ln2_b, n_new):
    row = lambda a: a[l][None, :]
    ws = w_s[l]
    tril = jnp.tril(ws[:, :n_new, :n_new])
    wv_s = jnp.repeat(tril.transpose(1, 2, 0), HEAD_DIM, axis=-1).reshape(n_new * n_new, D_SGU)
    bs_s = jnp.repeat(b_s[l][:, :n_new].T, HEAD_DIM, axis=-1)
    bs_full = jnp.repeat(b_s[l].T, HEAD_DIM, axis=-1)
    r_io = jnp.arange(CONV_W - 1)[None, :]
    i_io = jnp.arange(n_new)[:, None]
    tap = r_io - i_io
    wsh = jnp.where((tap >= 0)[..., None], w_dw[l][jnp.clip(tap, 0, CONV_W - 1)], 0.0)
    wr = w_router[l].T
    wr_hi = wr.astype(BF16)
    return dict(
        w_in=w_in[l].astype(BF16), w_out=w_out[l].astype(BF16), w_s=ws, bs_full=bs_full,
        sgu_g=row(sgu_ln_g), sgu_b=row(sgu_ln_b), w_dw=w_dw[l], b_dw=row(b_dw),
        cln_g=row(conv_ln_g), cln_b=row(conv_ln_b), wv_s=wv_s, bs_s=bs_s, wsh=wsh,
        ln1_g=row(ln1_g), ln1_b=row(ln1_b), wr_hi=wr_hi, wr_lo=(wr - wr_hi.astype(F32)).astype(BF16),
        b_router=b_router[l][:, None], ws_gu=w_sh_gu[l].astype(BF16), ws_dn=w_sh_dn[l].astype(BF16),
        ln2_g=row(ln2_g), ln2_b=row(ln2_b))


@jax.jit
def _forward(x_prompt, x_sample, cache_k, cache_v, page_table, state_conv, rel_bias, w_in, w_out,
             w_s, b_s, sgu_ln_g, sgu_ln_b, w_dw, b_dw, conv_ln_g, conv_ln_b, ln1_g, ln1_b,
             w_router, b_router, w_exp_gu, w_exp_dn, w_sh_gu, w_sh_dn, ln2_g, ln2_b):
    depth = w_in.shape[0]
    n_batch, seq, _ = x_prompt.shape
    n_dec, n_new, _ = x_sample.shape
    n_pool = cache_k.shape[1]
    past_len = page_table.shape[1] * PAGE_SIZE
    alpha = (2 * depth) ** 0.25
    rp = n_batch * seq
    rs = n_dec * n_new
    r = rp + rs
    tm = 512 if (seq % 512 == 0 and rs % 512 == 0) else MOBA_BLOCK
    assert seq % tm == 0 and rs % tm == 0 and seq % MOBA_BLOCK == 0 and seq // MOBA_BLOCK <= HEAD_DIM
    assert (seq // MOBA_BLOCK) % 8 == 0 and past_len % MOBA_BLOCK == 0 and n_dec % 8 == 0
    assert MAX_DISTANCE <= MOBA_BLOCK and r % TOK_TILE == 0
    assert (seq // MOBA_BLOCK) % (2 * FAR_GROUP) == 0 and seq >= 2 * MOBA_BLOCK
    del n_pool

    n_assign = r * TOP_K
    n_blk = (n_assign + N_EXPERTS * (EXPERT_ROWS - 1) + EXPERT_ROWS - 1) // EXPERT_ROWS
    rows_pad = n_blk * EXPERT_ROWS

    bt, t_last, t_own = _bias_tables(rel_bias, n_new, past_len)
    cache_kt = cache_k.transpose(0, 1, 3, 4, 2)
    cache_vt = cache_v.transpose(0, 1, 3, 4, 2)

    def to_heads(a):
        a = a.reshape(n_new, n_dec, H_ATT, HEAD_DIM).transpose(1, 2, 0, 3)
        return jnp.pad(a, ((0, 0), (0, 0), (0, NEW_PAD - n_new), (0, 0)))

    x = jnp.concatenate([x_prompt.reshape(rp, D_MODEL),
                         x_sample.transpose(1, 0, 2).reshape(rs, D_MODEL)], axis=0)
    xs_buf = jnp.zeros((rows_pad, HALF), U32)

    def to_bi(a):
        return a.reshape(n_new, n_dec, a.shape[-1]).transpose(1, 0, 2)

    outs = [[] for _ in range(8)]
    for l in range(depth):
        lw = _layer_weights(l, w_in, w_out, w_s, b_s, sgu_ln_g, sgu_ln_b, w_dw, b_dw, conv_ln_g,
                            conv_ln_b, ln1_g, ln1_b, w_router, b_router, w_sh_gu, w_sh_dn,
                            ln2_g, ln2_b, n_new)
        qb, kf, vf, ka, kb, vb, u_all, kmean = _inproj(x, lw["w_in"], tm, seq)
        kmean_p = kmean[:rp // MOBA_BLOCK].reshape(n_batch, seq // MOBA_BLOCK, D_ATT)
        att_p = _att_prompt(qb, ka, kb, vb, kmean_p, bt, n_batch, seq)
        k_s = to_bi(kf[rp:])
        v_s = to_bi(vf[rp:])
        att_s = _att_sample(page_table, to_heads(qb[rp:].astype(F32)), to_heads(kf[rp:]), to_heads(vf[rp:]),
                            cache_kt, cache_vt, l, t_last, t_own, n_new)
        att_s = att_s[:, :, :n_new].transpose(2, 0, 1, 3).reshape(rs, D_ATT).astype(BF16)
        sc_p, z_last, cbuf_p = _mix_prompt(u_all, lw, n_batch, seq, tm)
        sc_s, z_s, glu_s = _mix_sample(u_all[rp:], state_conv[l], lw, n_dec, n_new)
        att = jnp.concatenate([att_p, att_s], axis=0)
        sc = jnp.concatenate([sc_p, sc_s], axis=0)
        x1, x1p, eidx, rank, gw, cnt = _outproj(att, sc, x, lw, tm, alpha)
        counts = cnt[:, 0].astype(I32)
        padded = (counts + EXPERT_ROWS - 1) // EXPERT_ROWS * EXPERT_ROWS
        pend = jnp.cumsum(padded)
        pstart = pend - padded
        e_io = jnp.arange(N_EXPERTS, dtype=I32)
        pstart_of = jnp.sum(jnp.where(eidx[..., None] == e_io, pstart, 0), axis=-1)
        dest = (pstart_of + rank).T.reshape(r // TOK_TILE, TOK_TILE * TOP_K)
        blk_rows = jnp.arange(n_blk, dtype=I32)[:, None] * EXPERT_ROWS
        blk_e = jnp.minimum(jnp.sum((pend[None, :] <= blk_rows).astype(I32), axis=1), N_EXPERTS - 1)
        n_used = (pend[-1:] // EXPERT_ROWS).astype(I32)
        xs_buf = _dispatch(dest, x1p, xs_buf)
        ys = _experts(blk_e, n_used, xs_buf, w_exp_gu, w_exp_dn, l)
        x = _combine(dest, gw.T, x1, ys, lw, alpha)

        outs[0].append(kf[:rp].reshape(n_batch, seq, H_ATT, HEAD_DIM))
        outs[1].append(vf[:rp].reshape(n_batch, seq, H_ATT, HEAD_DIM))
        outs[2].append(k_s.reshape(n_dec, n_new, H_ATT, HEAD_DIM))
        outs[3].append(v_s.reshape(n_dec, n_new, H_ATT, HEAD_DIM))
        outs[4].append(z_last)
        outs[5].append(to_bi(z_s))
        outs[6].append(cbuf_p)
        outs[7].append(jnp.concatenate([state_conv[l], to_bi(glu_s)], axis=1)[:, -(CONV_W - 1):])
    y_prompt = x[:rp].reshape(n_batch, seq, D_MODEL)
    y_sample = to_bi(x[rp:])
    return (y_prompt, y_sample) + tuple(jnp.stack(o) for o in outs)


def kernel(x_prompt, x_sample, cache_k, cache_v, page_table, state_conv, rel_bias, w_in, w_out, w_s, b_s,
           sgu_ln_g, sgu_ln_b, w_dw, b_dw, conv_ln_g, conv_ln_b, ln1_g, ln1_b, w_router, b_router,
           w_exp_gu, w_exp_dn, w_sh_gu, w_sh_dn, ln2_g, ln2_b):
    return _forward(x_prompt, x_sample, cache_k, cache_v, page_table, state_conv, rel_bias, w_in, w_out,
                    w_s, b_s, sgu_ln_g, sgu_ln_b, w_dw, b_dw, conv_ln_g, conv_ln_b, ln1_g, ln1_b,
                    w_router, b_router, w_exp_gu, w_exp_dn, w_sh_gu, w_sh_dn, ln2_g, ln2_b)
```
